```python
import jax, jax.numpy as jnp
from jax import lax
import numpy as np

D_MODEL = 2048
BATCH = 16
SEQ = 2048
DEPTH = 4

N_MIXERS = 2
EPS = 1e-6
CHUNK = 128
GMLP_WIDTH = D_MODEL
GMLP_GROUP = 128
GMLP_GROUPS = GMLP_WIDTH // GMLP_GROUP
HEAD_DIM = 128
HEADS_PER_GROUP = D_MODEL // 256
DILATION_CONFIGS = ((128, 1), (512, 4), (2048, 16))
N_DIL = len(DILATION_CONFIGS)
ATTN_WIDTH = HEADS_PER_GROUP * HEAD_DIM
QKV_WIDTH = 3 * N_DIL * ATTN_WIDTH
ROT_DIM = HEAD_DIM // 4
ROPE_THETA = 500000.0
FFN_DIM = 5632
CONV_WIDTH = 3

kernel_name = "hybrid_gmlp_dilated_attn_convffn"


def rms_norm(x, g):
    xf = x.astype(jnp.float32)
    y = xf * lax.rsqrt(jnp.mean(xf * xf, axis=-1, keepdims=True) + EPS)
    return (y * g.astype(jnp.float32)).astype(x.dtype)


def layer_norm(x, g, b):
    xf = x.astype(jnp.float32)
    mu = jnp.mean(xf, axis=-1, keepdims=True)
    xc = xf - mu
    var = jnp.mean(xc * xc, axis=-1, keepdims=True)
    y = xc * lax.rsqrt(var + EPS) * g.astype(jnp.float32) + b.astype(jnp.float32)
    return y.astype(x.dtype)


def rotary_tables(seq):
    pos = jnp.arange(seq, dtype=jnp.float32)
    inv = jnp.float32(ROPE_THETA) ** (-jnp.arange(0, ROT_DIM, 2, dtype=jnp.float32) / ROT_DIM)
    ang = pos[:, None] * inv[None, :]
    return jnp.cos(ang), jnp.sin(ang)


def apply_partial_rope(t, cos, sin):
    half = ROT_DIM // 2
    c = cos[None, :, None, None, :].astype(t.dtype)
    s = sin[None, :, None, None, :].astype(t.dtype)
    t1 = t[..., :half]
    t2 = t[..., half:ROT_DIM]
    return jnp.concatenate([t1 * c - t2 * s, t1 * s + t2 * c, t[..., ROT_DIM:]], axis=-1)


def gmlp_chunk_mixer(h, w_in, ln_g, ln_b, w_s, b_s, w_out):
    B, S, _ = h.shape
    z = jax.nn.gelu(h @ w_in)
    u, v = jnp.split(z, 2, axis=-1)
    v = layer_norm(v, ln_g, ln_b)
    vc = v.reshape(B, S // CHUNK, CHUNK, GMLP_GROUPS, GMLP_GROUP)
    causal = jnp.tril(jnp.ones((CHUNK, CHUNK), dtype=bool))
    ws = jnp.where(causal[None], w_s, 0).astype(v.dtype)
    mixed = jnp.einsum('gts,bnsgc->bntgc', ws, vc) + b_s.T[None, None, :, :, None].astype(v.dtype)
    return (u * mixed.reshape(B, S, GMLP_WIDTH)) @ w_out


def dilated_group_attention(q, k, v, window, dilation):
    B, S, H, E = q.shape
    d = dilation
    wb = window // dilation
    L = S // d
    nb = -(-L // wb)
    Lp = nb * wb
    pad_end = ((0, 0), (0, Lp - L), (0, 0), (0, 0), (0, 0))
    pad_front = ((0, 0), (wb, 0), (0, 0), (0, 0), (0, 0))

    def strided(t):
        return jnp.pad(t.reshape(B, L, d, H, E), pad_end)

    def context(t):
        tp = jnp.pad(t, pad_front).reshape(B, nb + 1, wb, d, H, E)
        return jnp.concatenate([tp[:, :-1], tp[:, 1:]], axis=2)

    qb = strided(q).reshape(B, nb, wb, d, H, E)
    kc = context(strided(k))
    vc = context(strided(v))
    s = jnp.einsum('bnqrhe,bnkrhe->bnrhqk', qb, kc).astype(jnp.float32) * (HEAD_DIM ** -0.5)
    qi = jnp.arange(wb)[:, None]
    kj = jnp.arange(2 * wb)[None, :]
    dist = wb + qi - kj
    band = (dist >= 0) & (dist <= wb)
    key_sub = jnp.arange(nb)[:, None] * wb - wb + jnp.arange(2 * wb)[None, :]
    mask = band[None] & (key_sub >= 0)[:, None, :]
    s = jnp.where(mask[None, :, None, None], s, -jnp.inf)
    m = jnp.max(s, axis=-1, keepdims=True)
    p = jnp.exp(s - m)
    den = jnp.sum(p, axis=-1, keepdims=True)
    lse = (m + jnp.log(den))[..., 0]
    o = jnp.einsum('bnrhqk,bnkrhe->bnqrhe', p / den, vc.astype(jnp.float32))
    o = o.reshape(B, Lp, d, H, E)[:, :L].reshape(B, S, H, E)
    lse = jnp.transpose(lse, (0, 1, 4, 2, 3)).reshape(B, Lp, d, H)[:, :L].reshape(B, S, H)
    return o, lse


def dilated_attention_mixer(h, w_qkv, w_o, cos, sin):
    B, S, _ = h.shape
    qkv = (h @ w_qkv).reshape(B, S, 3, N_DIL, HEADS_PER_GROUP, HEAD_DIM)
    q = apply_partial_rope(qkv[:, :, 0], cos, sin)
    k = apply_partial_rope(qkv[:, :, 1], cos, sin)
    v = qkv[:, :, 2]
    outs, lses = [], []
    for g, (window, dilation) in enumerate(DILATION_CONFIGS):
        o_g, l_g = dilated_group_attention(q[:, :, g], k[:, :, g], v[:, :, g], window, dilation)
        outs.append(o_g)
        lses.append(l_g)
    wts = jax.nn.softmax(jnp.stack(lses), axis=0)
    o = jnp.einsum('gbsh,gbshe->bshe', wts, jnp.stack(outs))
    return o.reshape(B, S, ATTN_WIDTH).astype(h.dtype) @ w_o


def conv_ffn(h, w_gate, w_up, conv_w, conv_b, w_down):
    g = h @ w_gate
    g = lax.conv_general_dilated(
        g, conv_w[:, None, :].astype(g.dtype), window_strides=(1,),
        padding=[(CONV_WIDTH - 1, 0)], dimension_numbers=('NWC', 'WIO', 'NWC'),
        feature_group_count=FFN_DIM) + conv_b.astype(g.dtype)
    return (jax.nn.silu(g) * (h @ w_up)) @ w_down


def setup_inputs(seed: int = 0) -> dict:
    key = jax.random.key(seed)
    ks = jax.random.split(key, 24)
    n_a = (DEPTH + 1) // 2
    n_b = DEPTH // 2
    f32 = jnp.float32

    def dense(k, shape, fan_in):
        return jax.random.normal(k, shape, f32) * (fan_in ** -0.5)

    def gain(k, shape):
        return 1.0 + 0.02 * jax.random.normal(k, shape, f32)

    return {
        "x": jax.random.normal(ks[0], (BATCH, SEQ, D_MODEL), f32),
        "a_ln": gain(ks[1], (n_a, D_MODEL)),
        "a_w_in": dense(ks[2], (n_a, D_MODEL, 2 * GMLP_WIDTH), D_MODEL),
        "a_sgu_ln_g": gain(ks[3], (n_a, GMLP_WIDTH)),
        "a_sgu_ln_b": 0.02 * jax.random.normal(ks[4], (n_a, GMLP_WIDTH), f32),
        "a_w_s": dense(ks[5], (n_a, GMLP_GROUPS, CHUNK, CHUNK), CHUNK),
        "a_b_s": gain(ks[6], (n_a, GMLP_GROUPS, CHUNK)),
        "a_w_out": dense(ks[7], (n_a, GMLP_WIDTH, D_MODEL), GMLP_WIDTH),
        "b_ln": gain(ks[8], (n_b, D_MODEL)),
        "b_w_qkv": dense(ks[9], (n_b, D_MODEL, QKV_WIDTH), D_MODEL),
        "b_w_o": dense(ks[10], (n_b, ATTN_WIDTH, D_MODEL), ATTN_WIDTH),
        "ffn_ln": gain(ks[11], (DEPTH, D_MODEL)),
        "ffn_w_gate": dense(ks[12], (DEPTH, D_MODEL, FFN_DIM), D_MODEL),
        "ffn_w_up": dense(ks[13], (DEPTH, D_MODEL, FFN_DIM), D_MODEL),
        "ffn_conv_w": dense(ks[14], (DEPTH, CONV_WIDTH, FFN_DIM), CONV_WIDTH),
        "ffn_conv_b": 0.02 * jax.random.normal(ks[15], (DEPTH, FFN_DIM), f32),
        "ffn_w_down": dense(ks[16], (DEPTH, FFN_DIM, D_MODEL), FFN_DIM),
        "final_ln": gain(ks[17], (D_MODEL,)),
    }


def reference(x, a_ln, a_w_in, a_sgu_ln_g, a_sgu_ln_b, a_w_s, a_b_s, a_w_out,
              b_ln, b_w_qkv, b_w_o, ffn_ln, ffn_w_gate, ffn_w_up, ffn_conv_w,
              ffn_conv_b, ffn_w_down, final_ln):
    cos, sin = rotary_tables(x.shape[1])
    h = x
    for i in range(DEPTH):
        j = i // N_MIXERS
        if i % N_MIXERS == 0:
            h = h + gmlp_chunk_mixer(rms_norm(h, a_ln[j]), a_w_in[j], a_sgu_ln_g[j],
                                     a_sgu_ln_b[j], a_w_s[j], a_b_s[j], a_w_out[j])
        else:
            h = h + dilated_attention_mixer(rms_norm(h, b_ln[j]), b_w_qkv[j], b_w_o[j], cos, sin)
        h = h + conv_ffn(rms_norm(h, ffn_ln[i]), ffn_w_gate[i], ffn_w_up[i],
                         ffn_conv_w[i], ffn_conv_b[i], ffn_w_down[i])
    return rms_norm(h, final_ln)
```

```python
import functools

import jax
import jax.numpy as jnp
from jax import lax
from jax.experimental import pallas as pl
from jax.experimental.pallas import tpu as pltpu

F32 = jnp.float32
BF16 = jnp.bfloat16

EPS = 1e-6
CHUNK = 128
HEAD_DIM = 128
HEADS = 8
DILATIONS = (1, 4, 16)
ATTN_BLOCK = 128
ROT_DIM = 32
ROPE_THETA = 500000.0
ATTN_SCALE = HEAD_DIM ** -0.5

VMEM_LIMIT_BYTES = 56 * 1024 * 1024


def _params(n_axes):
    return pltpu.CompilerParams(
        dimension_semantics=("arbitrary",) * n_axes,
        vmem_limit_bytes=VMEM_LIMIT_BYTES,
    )


def _rms(x, gain):
    ms = jnp.mean(x * x, axis=-1, keepdims=True)
    return x * lax.rsqrt(ms + EPS) * gain


def _gelu_tanh(x):
    c = 0.7978845608028654
    return 0.5 * x * (1.0 + jnp.tanh(c * (x + 0.044715 * (x * x * x))))


def _rms_proj_kernel(h_ref, gain_ref, w_ref, o_ref, hn_ref, *, gelu):
    @pl.when(pl.program_id(1) == 0)
    def _():
        hn_ref[...] = _rms(h_ref[...], gain_ref[...]).astype(BF16)

    y = jnp.dot(hn_ref[...], w_ref[...], preferred_element_type=F32)
    if gelu:
        y = _gelu_tanh(y)
    o_ref[...] = y.astype(o_ref.dtype)


def _rms_proj(h, gain, w, *, gelu, tm=1024, tn=1024):
    t, d = h.shape
    n = w.shape[1]
    return pl.pallas_call(
        functools.partial(_rms_proj_kernel, gelu=gelu),
        grid=(t // tm, n // tn),
        in_specs=[
            pl.BlockSpec((tm, d), lambda i, j: (i, 0)),
            pl.BlockSpec((1, d), lambda i, j: (0, 0)),
            pl.BlockSpec((d, tn), lambda i, j: (0, j)),
        ],
        out_specs=pl.BlockSpec((tm, tn), lambda i, j: (i, j)),
        out_shape=jax.ShapeDtypeStruct((t, n), BF16),
        scratch_shapes=[pltpu.VMEM((tm, d), BF16)],
        compiler_params=_params(2),
        name="rms_proj_gelu" if gelu else "rms_proj",
    )(h, gain, w)


def _gmlp_mix_kernel(u_ref, v_ref, lng_ref, lnb_ref, ws_ref, bst_ref, wout_ref,
                     h_ref, o_ref, gated_ref, *, tm):
    n_chunks = tm // CHUNK
    groups = ws_ref.shape[0]

    v = v_ref[...].astype(F32)
    mu = jnp.mean(v, axis=-1, keepdims=True)
    vc = v - mu
    var = jnp.mean(vc * vc, axis=-1, keepdims=True)
    vn = (vc * lax.rsqrt(var + EPS) * lng_ref[...] + lnb_ref[...]).astype(BF16)

    row = lax.broadcasted_iota(jnp.int32, (CHUNK, CHUNK), 0)
    col = lax.broadcasted_iota(jnp.int32, (CHUNK, CHUNK), 1)
    causal = col <= row
    for g in range(groups):
        cols = slice(g * CHUNK, (g + 1) * CHUNK)
        rhs = jnp.concatenate(
            [vn[n * CHUNK:(n + 1) * CHUNK, cols] for n in range(n_chunks)], axis=1)
        ws = jnp.where(causal, ws_ref[g], 0.0).astype(BF16)
        mixed = jnp.dot(ws, rhs, preferred_element_type=F32) + bst_ref[:, g:g + 1]
        for n in range(n_chunks):
            rows = slice(n * CHUNK, (n + 1) * CHUNK)
            u = u_ref[rows, cols].astype(F32)
            gated_ref[rows, cols] = (u * mixed[:, n * CHUNK:(n + 1) * CHUNK]).astype(BF16)

    o_ref[...] = h_ref[...] + jnp.dot(gated_ref[...], wout_ref[...],
                                      preferred_element_type=F32)


def _gmlp_mix(z, ln_g, ln_b, w_s, b_s_t, w_out, h, *, tm=512):
    t, d = h.shape
    width = w_out.shape[0]
    groups = w_s.shape[0]
    return pl.pallas_call(
        functools.partial(_gmlp_mix_kernel, tm=tm),
        grid=(t // tm,),
        in_specs=[
            pl.BlockSpec((tm, width), lambda i: (i, 0)),
            pl.BlockSpec((tm, width), lambda i: (i, 1)),
            pl.BlockSpec((1, width), lambda i: (0, 0)),
            pl.BlockSpec((1, width), lambda i: (0, 0)),
            pl.BlockSpec((groups, CHUNK, CHUNK), lambda i: (0, 0, 0)),
            pl.BlockSpec((CHUNK, groups), lambda i: (0, 0)),
            pl.BlockSpec((width, d), lambda i: (0, 0)),
            pl.BlockSpec((tm, d), lambda i: (i, 0)),
        ],
        out_specs=pl.BlockSpec((tm, d), lambda i: (i, 0)),
        out_shape=jax.ShapeDtypeStruct((t, d), F32),
        scratch_shapes=[pltpu.VMEM((tm, width), BF16)],
        compiler_params=_params(1),
        name="gmlp_mix",
    )(z, z, ln_g, ln_b, w_s, b_s_t, w_out, h)


def _attn_block(q_sc, k_sc, v_sc, acc_sc, m_sc, den_sc, d, r, n):
    w = ATTN_BLOCK

    def rows(first_block, count):
        start = r + d * w * first_block
        return pl.ds(start, count) if d == 1 else pl.ds(start, count, stride=d)

    q_rows = rows(n, w)
    k_rows = q_rows if n == 0 else rows(n - 1, 2 * w)
    nk = w if n == 0 else 2 * w
    q = q_sc[q_rows, :].astype(BF16)
    k = k_sc[k_rows, :].astype(BF16)
    v = v_sc[k_rows, :].astype(BF16)
    s = lax.dot_general(q, k, (((1,), (1,)), ((), ())),
                        preferred_element_type=F32) * ATTN_SCALE
    qi = lax.broadcasted_iota(jnp.int32, (w, nk), 0)
    kj = lax.broadcasted_iota(jnp.int32, (w, nk), 1)
    valid = (kj <= qi) if n == 0 else ((kj >= qi) & (kj <= qi + w))
    s = jnp.where(valid, s, -jnp.inf)
    m = jnp.max(s, axis=-1, keepdims=True)
    p = jnp.exp(s - m)
    den = jnp.sum(p, axis=-1, keepdims=True)
    acc_sc[q_rows, :] = jnp.dot(p.astype(BF16), v, preferred_element_type=F32)
    m_sc[q_rows, :] = jnp.broadcast_to(m, (w, HEAD_DIM))
    den_sc[q_rows, :] = jnp.broadcast_to(den, (w, HEAD_DIM))


def _attn_kernel(*refs, seq):
    qkv_refs = refs[:9]
    cos_ref, sin_lo_ref, sin_hi_ref, o_ref = refs[9:13]
    scratch = refs[13:]
    rope_rows = 256

    for g, d in enumerate(DILATIONS):
        q_ref, k_ref, v_ref = qkv_refs[3 * g:3 * g + 3]
        q_sc, k_sc, v_sc, acc_sc, m_sc, den_sc = scratch[6 * g:6 * g + 6]

        def rope_body(c, carry, q_ref=q_ref, k_ref=k_ref, v_ref=v_ref,
                      q_sc=q_sc, k_sc=k_sc, v_sc=v_sc):
            rows = pl.ds(pl.multiple_of(c * rope_rows, rope_rows), rope_rows)
            cos = cos_ref[rows, :]
            sin_lo = sin_lo_ref[rows, :]
            sin_hi = sin_hi_ref[rows, :]
            for src, dst in ((q_ref, q_sc), (k_ref, k_sc)):
                t = src[rows, :].astype(F32)
                dst[rows, :] = (t * cos
                                + pltpu.roll(t, HEAD_DIM - ROT_DIM // 2, 1) * sin_lo
                                + pltpu.roll(t, ROT_DIM // 2, 1) * sin_hi)
            v_sc[rows, :] = v_ref[rows, :].astype(F32)
            return carry

        lax.fori_loop(0, seq // rope_rows, rope_body, 0)

        blocks = seq // d // ATTN_BLOCK
        for r in range(d):
            for n in range(blocks):
                _attn_block(q_sc, k_sc, v_sc, acc_sc, m_sc, den_sc, d, r, n)

    def merge_body(c, carry):
        rows = pl.ds(pl.multiple_of(c * rope_rows, rope_rows), rope_rows)
        ms = [scratch[6 * g + 4][rows, :] for g in range(3)]
        m_all = jnp.maximum(jnp.maximum(ms[0], ms[1]), ms[2])
        num = jnp.zeros((rope_rows, HEAD_DIM), F32)
        den = jnp.zeros((rope_rows, HEAD_DIM), F32)
        for g in range(3):
            wgt = jnp.exp(ms[g] - m_all)
            num = num + wgt * scratch[6 * g + 3][rows, :]
            den = den + wgt * scratch[6 * g + 5][rows, :]
        o_ref[rows, :] = (num / den).astype(o_ref.dtype)
        return carry

    lax.fori_loop(0, seq // rope_rows, merge_body, 0)


def _attention(qkv, cos_t, sin_lo_t, sin_hi_t):
    b, s, _ = qkv.shape
    n_dil = len(DILATIONS)

    def qkv_spec(part, g):
        base = (part * n_dil + g) * HEADS
        return pl.BlockSpec((None, s, HEAD_DIM), lambda bi, hi: (bi, 0, base + hi))

    table_spec = pl.BlockSpec((s, HEAD_DIM), lambda bi, hi: (0, 0))
    in_specs = [qkv_spec(part, g) for g in range(n_dil) for part in range(3)]
    in_specs += [table_spec] * 3
    return pl.pallas_call(
        functools.partial(_attn_kernel, seq=s),
        grid=(b, HEADS),
        in_specs=in_specs,
        out_specs=pl.BlockSpec((None, s, HEAD_DIM), lambda bi, hi: (bi, 0, hi)),
        out_shape=jax.ShapeDtypeStruct((b, s, HEADS * HEAD_DIM), BF16),
        scratch_shapes=[pltpu.VMEM((s, HEAD_DIM), F32)] * (6 * n_dil),
        compiler_params=_params(2),
        name="dilated_attn",
    )(*([qkv] * 9), cos_t, sin_lo_t, sin_hi_t)


def _out_proj_kernel(a_ref, w_ref, h_ref, o_ref):
    o_ref[...] = h_ref[...] + jnp.dot(a_ref[...], w_ref[...], preferred_element_type=F32)


def _out_proj(a, w, h, *, tm=512):
    t, d = h.shape
    k = a.shape[1]
    return pl.pallas_call(
        _out_proj_kernel,
        grid=(t // tm,),
        in_specs=[
            pl.BlockSpec((tm, k), lambda i: (i, 0)),
            pl.BlockSpec((k, d), lambda i: (0, 0)),
            pl.BlockSpec((tm, d), lambda i: (i, 0)),
        ],
        out_specs=pl.BlockSpec((tm, d), lambda i: (i, 0)),
        out_shape=jax.ShapeDtypeStruct((t, d), F32),
        compiler_params=_params(1),
        name="attn_out_proj",
    )(a, w, h)


def _ffn_kernel(h_ref, gain_ref, wg_ref, wu_ref, cw_ref, cb_ref, wd_ref, fin_ref,
                o_ref, hn_ref, gbuf_ref, halo_ref, *, tm, seq, final_norm):
    i = pl.program_id(0)
    j = pl.program_id(1)

    @pl.when(j == 0)
    def _():
        x = h_ref[...]
        hn_ref[...] = _rms(x, gain_ref[...]).astype(BF16)
        o_ref[...] = x

    hn = hn_ref[...]
    g = jnp.dot(hn, wg_ref[...], preferred_element_type=F32)
    u = jnp.dot(hn, wu_ref[...], preferred_element_type=F32)

    seq_start = (i * tm) % seq == 0
    gbuf_ref[0:8, :] = jnp.where(seq_start, 0.0, halo_ref[j])
    gbuf_ref[8:8 + tm, :] = g
    conv = (cw_ref[2:3, :] * g
            + cw_ref[1:2, :] * gbuf_ref[7:7 + tm, :]
            + cw_ref[0:1, :] * gbuf_ref[6:6 + tm, :]
            + cb_ref[...])
    act = (conv * (1.0 / (1.0 + jnp.exp(-conv))) * u).astype(BF16)
    o_ref[...] += jnp.dot(act, wd_ref[...], preferred_element_type=F32)
    halo_ref[j] = gbuf_ref[tm:tm + 8, :]

    if final_norm:
        @pl.when(j == pl.num_programs(1) - 1)
        def _():
            o_ref[...] = _rms(o_ref[...], fin_ref[...])


def _ffn(h, gain, w_gate, w_up, conv_w, conv_b, w_down, final_gain, *, seq,
         final_norm, tm=512, tf=512):
    t, d = h.shape
    f = w_gate.shape[1]
    return pl.pallas_call(
        functools.partial(_ffn_kernel, tm=tm, seq=seq, final_norm=final_norm),
        grid=(t // tm, f // tf),
        in_specs=[
            pl.BlockSpec((tm, d), lambda i, j: (i, 0)),
            pl.BlockSpec((1, d), lambda i, j: (0, 0)),
            pl.BlockSpec((d, tf), lambda i, j: (0, j)),
            pl.BlockSpec((d, tf), lambda i, j: (0, j)),
            pl.BlockSpec((conv_w.shape[0], tf), lambda i, j: (0, j)),
            pl.BlockSpec((1, tf), lambda i, j: (0, j)),
            pl.BlockSpec((tf, d), lambda i, j: (j, 0)),
            pl.BlockSpec((1, d), lambda i, j: (0, 0)),
        ],
        out_specs=pl.BlockSpec((tm, d), lambda i, j: (i, 0)),
        out_shape=jax.ShapeDtypeStruct((t, d), F32),
        scratch_shapes=[
            pltpu.VMEM((tm, d), BF16),
            pltpu.VMEM((tm + 8, tf), F32),
            pltpu.VMEM((f // tf, 8, tf), F32),
        ],
        compiler_params=_params(2),
        name="conv_ffn",
    )(h, gain, w_gate, w_up, conv_w, conv_b, w_down, final_gain)


def _rope_tables(seq):
    half = ROT_DIM // 2
    pos = jnp.arange(seq, dtype=F32)
    inv = jnp.float32(ROPE_THETA) ** (-jnp.arange(0, ROT_DIM, 2, dtype=F32) / ROT_DIM)
    ang = pos[:, None] * inv[None, :]
    cos, sin = jnp.cos(ang), jnp.sin(ang)
    pad = jnp.zeros((seq, HEAD_DIM - ROT_DIM), F32)
    zero = jnp.zeros((seq, half), F32)
    cos_t = jnp.concatenate([cos, cos, pad + 1.0], axis=1)
    sin_lo_t = jnp.concatenate([-sin, zero, pad], axis=1)
    sin_hi_t = jnp.concatenate([zero, sin, pad], axis=1)
    return cos_t, sin_lo_t, sin_hi_t


def kernel(x, a_ln, a_w_in, a_sgu_ln_g, a_sgu_ln_b, a_w_s, a_b_s, a_w_out, b_ln, b_w_qkv, b_w_o, ffn_ln, ffn_w_gate, ffn_w_up, ffn_conv_w, ffn_conv_b, ffn_w_down, final_ln):
    b, s, d = x.shape
    depth = ffn_ln.shape[0]
    h = x.reshape(b * s, d)
    tables = _rope_tables(s)
    for i in range(depth):
        j = i // 2
        if i % 2 == 0:
            z = _rms_proj(h, a_ln[j][None], a_w_in[j].astype(BF16), gelu=True)
            h = _gmlp_mix(z, a_sgu_ln_g[j][None], a_sgu_ln_b[j][None], a_w_s[j],
                          a_b_s[j].T, a_w_out[j].astype(BF16), h)
        else:
            qkv = _rms_proj(h, b_ln[j][None], b_w_qkv[j].astype(BF16), gelu=False)
            att = _attention(qkv.reshape(b, s, -1), *tables)
            h = _out_proj(att.reshape(b * s, -1), b_w_o[j].astype(BF16), h)
        h = _ffn(h, ffn_ln[i][None], ffn_w_gate[i].astype(BF16), ffn_w_up[i].astype(BF16),
                 ffn_conv_w[i], ffn_conv_b[i][None], ffn_w_down[i].astype(BF16),
                 final_ln[None], seq=s, final_norm=(i == depth - 1))
    return h.reshape(b, s, d)
```

```python
import functools

import jax
import jax.numpy as jnp
from jax import lax
from jax.experimental import pallas as pl
from jax.experimental.pallas import tpu as pltpu

F32 = jnp.float32
BF16 = jnp.bfloat16

EPS = 1e-6
CHUNK = 128
HEAD_DIM = 128
HEADS = 8
DILATIONS = (1, 4, 16)
ATTN_BLOCK = 128
ROT_DIM = 32
ROPE_THETA = 500000.0
ATTN_SCALE = HEAD_DIM ** -0.5
LOG2_E = 1.4426950408889634
MXU_COLS = 256

VMEM_LIMIT_BYTES = 56 * 1024 * 1024


def _params(n_axes):
    return pltpu.CompilerParams(
        dimension_semantics=("arbitrary",) * n_axes,
        vmem_limit_bytes=VMEM_LIMIT_BYTES,
    )


def _rms(x, gain):
    ms = jnp.mean(x * x, axis=-1, keepdims=True)
    return x * lax.rsqrt(ms + EPS) * gain


def _gelu_tanh(x):
    c = 0.7978845608028654
    return 0.5 * x * (1.0 + jnp.tanh(c * (x + 0.044715 * (x * x * x))))


def _rms_gelu_proj_kernel(h_ref, gain_ref, w_ref, o_ref, hn_ref):
    @pl.when(pl.program_id(1) == 0)
    def _():
        hn_ref[...] = _rms(h_ref[...], gain_ref[...]).astype(BF16)

    hn = hn_ref[...]
    for c in range(w_ref.shape[1] // MXU_COLS):
        cols = slice(c * MXU_COLS, (c + 1) * MXU_COLS)
        y = jnp.dot(hn, w_ref[:, cols], preferred_element_type=F32)
        o_ref[:, cols] = _gelu_tanh(y).astype(o_ref.dtype)


def _rms_gelu_proj(h, gain, w, *, tm=1024, tn=1024):
    t, d = h.shape
    n = w.shape[1]
    return pl.pallas_call(
        _rms_gelu_proj_kernel,
        grid=(t // tm, n // tn),
        in_specs=[
            pl.BlockSpec((tm, d), lambda i, j: (i, 0)),
            pl.BlockSpec((1, d), lambda i, j: (0, 0)),
            pl.BlockSpec((d, tn), lambda i, j: (0, j)),
        ],
        out_specs=pl.BlockSpec((tm, tn), lambda i, j: (i, j)),
        out_shape=jax.ShapeDtypeStruct((t, n), BF16),
        scratch_shapes=[pltpu.VMEM((tm, d), BF16)],
        compiler_params=_params(2),
        name="rms_gelu_proj",
    )(h, gain, w)


def _gmlp_mix_kernel(u_ref, v_ref, lng_ref, lnb_ref, ws_ref, bst_ref, wout_ref,
                     h_ref, o_ref, gated_ref, *, tm):
    n_chunks = tm // CHUNK
    groups = ws_ref.shape[0]

    v = v_ref[...].astype(F32)
    mu = jnp.mean(v, axis=-1, keepdims=True)
    vc = v - mu
    var = jnp.mean(vc * vc, axis=-1, keepdims=True)
    vn = (vc * lax.rsqrt(var + EPS) * lng_ref[...] + lnb_ref[...]).astype(BF16)

    row = lax.broadcasted_iota(jnp.int32, (CHUNK, CHUNK), 0)
    col = lax.broadcasted_iota(jnp.int32, (CHUNK, CHUNK), 1)
    causal = col <= row
    for g in range(groups):
        cols = slice(g * CHUNK, (g + 1) * CHUNK)
        rhs = jnp.concatenate(
            [vn[n * CHUNK:(n + 1) * CHUNK, cols] for n in range(n_chunks)], axis=1)
        ws = jnp.where(causal, ws_ref[g], 0.0).astype(BF16)
        mixed = jnp.dot(ws, rhs, preferred_element_type=F32) + bst_ref[:, g:g + 1]
        for n in range(n_chunks):
            rows = slice(n * CHUNK, (n + 1) * CHUNK)
            u = u_ref[rows, cols].astype(F32)
            gated_ref[rows, cols] = (u * mixed[:, n * CHUNK:(n + 1) * CHUNK]).astype(BF16)

    o_ref[...] = h_ref[...] + jnp.dot(gated_ref[...], wout_ref[...],
                                      preferred_element_type=F32)


def _gmlp_mix(z, ln_g, ln_b, w_s, b_s_t, w_out, h, *, tm=512):
    t, d = h.shape
    width = w_out.shape[0]
    groups = w_s.shape[0]
    return pl.pallas_call(
        functools.partial(_gmlp_mix_kernel, tm=tm),
        grid=(t // tm,),
        in_specs=[
            pl.BlockSpec((tm, width), lambda i: (i, 0)),
            pl.BlockSpec((tm, width), lambda i: (i, 1)),
            pl.BlockSpec((1, width), lambda i: (0, 0)),
            pl.BlockSpec((1, width), lambda i: (0, 0)),
            pl.BlockSpec((groups, CHUNK, CHUNK), lambda i: (0, 0, 0)),
            pl.BlockSpec((CHUNK, groups), lambda i: (0, 0)),
            pl.BlockSpec((width, d), lambda i: (0, 0)),
            pl.BlockSpec((tm, d), lambda i: (i, 0)),
        ],
        out_specs=pl.BlockSpec((tm, d), lambda i: (i, 0)),
        out_shape=jax.ShapeDtypeStruct((t, d), F32),
        scratch_shapes=[pltpu.VMEM((tm, width), BF16)],
        compiler_params=_params(1),
        name="gmlp_mix",
    )(z, z, ln_g, ln_b, w_s, b_s_t, w_out, h)


def _rms_qkv_kernel(h_ref, gain_ref, w_ref, cos_ref, sin_lo_ref, sin_hi_ref, o_ref,
                    hn_ref, *, d, d_model):
    rows = h_ref.shape[0]

    @pl.when(pl.program_id(1) == 0)
    def _():
        for r in range(d):
            x = h_ref[:, r * d_model:(r + 1) * d_model]
            hn_ref[r * rows:(r + 1) * rows, :] = _rms(x, gain_ref[...]).astype(BF16)

    hn = hn_ref[...]
    heads_per_chunk = MXU_COLS // HEAD_DIM
    for c in range(w_ref.shape[1] // MXU_COLS):
        y = jnp.dot(hn, w_ref[:, c * MXU_COLS:(c + 1) * MXU_COLS],
                    preferred_element_type=F32)
        for r in range(d):
            lanes = slice(r * HEAD_DIM, (r + 1) * HEAD_DIM)
            cos, sin_lo, sin_hi = cos_ref[:, lanes], sin_lo_ref[:, lanes], sin_hi_ref[:, lanes]
            for hh in range(heads_per_chunk):
                t = y[r * rows:(r + 1) * rows, hh * HEAD_DIM:(hh + 1) * HEAD_DIM]
                t = (t * cos + pltpu.roll(t, HEAD_DIM - ROT_DIM // 2, 1) * sin_lo
                     + pltpu.roll(t, ROT_DIM // 2, 1) * sin_hi)
                col0 = c * MXU_COLS + hh * HEAD_DIM
                o_ref[r, :, col0:col0 + HEAD_DIM] = t.astype(o_ref.dtype)


def _rms_qkv(h, gain, w_qkv, tables, *, batch, seq, g, tm=1024):
    d = DILATIONS[g]
    t, d_model = h.shape
    part = HEADS * HEAD_DIM
    n_dil = len(DILATIONS)
    tiles_per_seq = seq // tm
    rows = tm // d
    h_view = h.reshape(t // d, d * d_model)
    table_spec = pl.BlockSpec((None, rows, d * HEAD_DIM),
                              lambda i, j: (j // 2, i % tiles_per_seq, 0))
    return pl.pallas_call(
        functools.partial(_rms_qkv_kernel, d=d, d_model=d_model),
        grid=(t // tm, 3),
        in_specs=[
            pl.BlockSpec((rows, d * d_model), lambda i, j: (i, 0)),
            pl.BlockSpec((1, d_model), lambda i, j: (0, 0)),
            pl.BlockSpec((d_model, part), lambda i, j: (0, j * n_dil + g)),
            table_spec, table_spec, table_spec,
        ],
        out_specs=pl.BlockSpec((None, d, rows, part),
                               lambda i, j: (i // tiles_per_seq, 0, i % tiles_per_seq, j)),
        out_shape=jax.ShapeDtypeStruct((batch, d, seq // d, 3 * part), BF16),
        scratch_shapes=[pltpu.VMEM((tm, d_model), BF16)],
        compiler_params=_params(2),
        name=f"rms_qkv_d{d}",
    )(h_view, gain, w_qkv, *[tb.reshape(2, seq // d, d * HEAD_DIM) for tb in tables])


def _attn_block(q_ref, k_ref, v_ref, bias_ref, acc_sc, m_sc, den_sc, d, r, n):
    w = ATTN_BLOCK
    q = q_ref[r, n * w:(n + 1) * w, :]
    first = max(n - 1, 0)
    k = k_ref[r, first * w:(n + 1) * w, :]
    v = v_ref[r, first * w:(n + 1) * w, :]
    nk = k.shape[0]
    s = lax.dot_general(q, k, (((1,), (1,)), ((), ())), preferred_element_type=F32)
    s = s + bias_ref[:, 2 * w - nk:]
    m = jnp.max(s, axis=-1, keepdims=True)
    p = jnp.exp2((s - m) * (ATTN_SCALE * LOG2_E)).astype(BF16)
    v_ones = jnp.concatenate([v, jnp.ones((nk, HEAD_DIM), BF16)], axis=1)
    pv = jnp.dot(p, v_ones, preferred_element_type=F32)
    out_rows = pl.ds(n * w, w) if d == 1 else pl.ds(r + d * w * n, w, stride=d)
    acc_sc[out_rows, :] = pv[:, :HEAD_DIM]
    den_sc[out_rows, :] = pv[:, HEAD_DIM:]
    m_sc[out_rows, :] = jnp.broadcast_to(m, (w, HEAD_DIM))


def _attn_kernel(*refs, seq):
    n_dil = len(DILATIONS)
    qkv_refs = refs[:3 * n_dil]
    o_ref = refs[3 * n_dil]
    bias_ref = refs[3 * n_dil + 1]
    stats = refs[3 * n_dil + 2:]
    w = ATTN_BLOCK

    qi = lax.broadcasted_iota(jnp.int32, (w, 2 * w), 0)
    kj = lax.broadcasted_iota(jnp.int32, (w, 2 * w), 1)
    bias_ref[...] = jnp.where((kj >= qi) & (kj <= qi + w), 0.0, -jnp.inf)

    for g, d in enumerate(DILATIONS):
        q_ref, k_ref, v_ref = qkv_refs[3 * g:3 * g + 3]
        acc_sc, m_sc, den_sc = stats[3 * g:3 * g + 3]
        for r in range(d):
            for n in range(seq // d // w):
                _attn_block(q_ref, k_ref, v_ref, bias_ref, acc_sc, m_sc, den_sc, d, r, n)

    merge_rows = 256

    def merge_body(c, carry):
        rows = pl.ds(pl.multiple_of(c * merge_rows, merge_rows), merge_rows)
        ms = [stats[3 * g + 1][rows, :] for g in range(n_dil)]
        m_all = functools.reduce(jnp.maximum, ms)
        num = jnp.zeros((merge_rows, HEAD_DIM), F32)
        den = jnp.zeros((merge_rows, HEAD_DIM), F32)
        for g in range(n_dil):
            wgt = jnp.exp2((ms[g] - m_all) * (ATTN_SCALE * LOG2_E))
            num = num + wgt * stats[3 * g][rows, :]
            den = den + wgt * stats[3 * g + 2][rows, :]
        o_ref[rows, :] = (num / den).astype(o_ref.dtype)
        return carry

    lax.fori_loop(0, seq // merge_rows, merge_body, 0)


def _attention(qkv_groups, *, seq):
    batch = qkv_groups[0].shape[0]
    n_dil = len(DILATIONS)
    in_specs = []
    for g, d in enumerate(DILATIONS):
        for part in range(3):
            in_specs.append(pl.BlockSpec(
                (None, d, seq // d, HEAD_DIM),
                lambda bi, hi, part=part: (bi, 0, 0, part * HEADS + hi)))
    operands = [qkv_groups[g] for g in range(n_dil) for _ in range(3)]
    return pl.pallas_call(
        functools.partial(_attn_kernel, seq=seq),
        grid=(batch, HEADS),
        in_specs=in_specs,
        out_specs=pl.BlockSpec((None, seq, HEAD_DIM), lambda bi, hi: (bi, 0, hi)),
        out_shape=jax.ShapeDtypeStruct((batch, seq, HEADS * HEAD_DIM), BF16),
        scratch_shapes=[pltpu.VMEM((ATTN_BLOCK, 2 * ATTN_BLOCK), F32)]
        + [pltpu.VMEM((seq, HEAD_DIM), F32)] * (3 * n_dil),
        compiler_params=_params(2),
        name="dilated_attn",
    )(*operands)


def _out_proj_kernel(a_ref, w_ref, h_ref, o_ref):
    o_ref[...] = h_ref[...] + jnp.dot(a_ref[...], w_ref[...], preferred_element_type=F32)


def _out_proj(a, w, h, *, tm=512):
    t, d = h.shape
    k = a.shape[1]
    return pl.pallas_call(
        _out_proj_kernel,
        grid=(t // tm,),
        in_specs=[
            pl.BlockSpec((tm, k), lambda i: (i, 0)),
            pl.BlockSpec((k, d), lambda i: (0, 0)),
            pl.BlockSpec((tm, d), lambda i: (i, 0)),
        ],
        out_specs=pl.BlockSpec((tm, d), lambda i: (i, 0)),
        out_shape=jax.ShapeDtypeStruct((t, d), F32),
        compiler_params=_params(1),
        name="attn_out_proj",
    )(a, w, h)


def _ffn_kernel(h_ref, gain_ref, wg_ref, wu_ref, cw_ref, cb_ref, wd_ref, fin_ref,
                o_ref, hn_ref, gbuf_ref, halo_ref, *, tm, seq, final_norm):
    i = pl.program_id(0)
    j = pl.program_id(1)

    @pl.when(j == 0)
    def _():
        x = h_ref[...]
        hn_ref[...] = _rms(x, gain_ref[...]).astype(BF16)
        o_ref[...] = x

    hn = hn_ref[...]
    g = jnp.dot(hn, wg_ref[...], preferred_element_type=F32)
    u = jnp.dot(hn, wu_ref[...], preferred_element_type=F32)

    seq_start = (i * tm) % seq == 0
    gbuf_ref[0:8, :] = jnp.where(seq_start, 0.0, halo_ref[j])
    gbuf_ref[8:8 + tm, :] = g
    conv = (cw_ref[2:3, :] * g
            + cw_ref[1:2, :] * gbuf_ref[7:7 + tm, :]
            + cw_ref[0:1, :] * gbuf_ref[6:6 + tm, :]
            + cb_ref[...])
    act = (conv * (1.0 / (1.0 + jnp.exp(-conv))) * u).astype(BF16)
    o_ref[...] += jnp.dot(act, wd_ref[...], preferred_element_type=F32)
    halo_ref[j] = gbuf_ref[tm:tm + 8, :]

    if final_norm:
        @pl.when(j == pl.num_programs(1) - 1)
        def _():
            o_ref[...] = _rms(o_ref[...], fin_ref[...])


def _ffn(h, gain, w_gate, w_up, conv_w, conv_b, w_down, final_gain, *, seq,
         final_norm, tm=512, tf=512):
    t, d = h.shape
    f = w_gate.shape[1]
    return pl.pallas_call(
        functools.partial(_ffn_kernel, tm=tm, seq=seq, final_norm=final_norm),
        grid=(t // tm, f // tf),
        in_specs=[
            pl.BlockSpec((tm, d), lambda i, j: (i, 0)),
            pl.BlockSpec((1, d), lambda i, j: (0, 0)),
            pl.BlockSpec((d, tf), lambda i, j: (0, j)),
            pl.BlockSpec((d, tf), lambda i, j: (0, j)),
            pl.BlockSpec((conv_w.shape[0], tf), lambda i, j: (0, j)),
            pl.BlockSpec((1, tf), lambda i, j: (0, j)),
            pl.BlockSpec((tf, d), lambda i, j: (j, 0)),
            pl.BlockSpec((1, d), lambda i, j: (0, 0)),
        ],
        out_specs=pl.BlockSpec((tm, d), lambda i, j: (i, 0)),
        out_shape=jax.ShapeDtypeStruct((t, d), F32),
        scratch_shapes=[
            pltpu.VMEM((tm, d), BF16),
            pltpu.VMEM((tm + 8, tf), F32),
            pltpu.VMEM((f // tf, 8, tf), F32),
        ],
        compiler_params=_params(2),
        name="conv_ffn",
    )(h, gain, w_gate, w_up, conv_w, conv_b, w_down, final_gain)


def _rope_tables(seq):
    half = ROT_DIM // 2
    pos = jnp.arange(seq, dtype=F32)
    inv = jnp.float32(ROPE_THETA) ** (-jnp.arange(0, ROT_DIM, 2, dtype=F32) / ROT_DIM)
    ang = pos[:, None] * inv[None, :]
    cos, sin = jnp.cos(ang), jnp.sin(ang)
    pad = jnp.zeros((seq, HEAD_DIM - ROT_DIM), F32)
    zero = jnp.zeros((seq, half), F32)
    cos_t = jnp.concatenate([cos, cos, pad + 1.0], axis=1)
    sin_lo_t = jnp.concatenate([-sin, zero, pad], axis=1)
    sin_hi_t = jnp.concatenate([zero, sin, pad], axis=1)
    return (jnp.stack([cos_t, jnp.ones_like(cos_t)]),
            jnp.stack([sin_lo_t, jnp.zeros_like(cos_t)]),
            jnp.stack([sin_hi_t, jnp.zeros_like(cos_t)]))


def kernel(x, a_ln, a_w_in, a_sgu_ln_g, a_sgu_ln_b, a_w_s, a_b_s, a_w_out, b_ln, b_w_qkv, b_w_o, ffn_ln, ffn_w_gate, ffn_w_up, ffn_conv_w, ffn_conv_b, ffn_w_down, final_ln):
    b, s, d = x.shape
    depth = ffn_ln.shape[0]
    h = x.reshape(b * s, d)
    tables = _rope_tables(s)
    for i in range(depth):
        j = i // 2
        if i % 2 == 0:
            z = _rms_gelu_proj(h, a_ln[j][None], a_w_in[j].astype(BF16))
            h = _gmlp_mix(z, a_sgu_ln_g[j][None], a_sgu_ln_b[j][None], a_w_s[j],
                          a_b_s[j].T, a_w_out[j].astype(BF16), h)
        else:
            w_qkv = b_w_qkv[j].astype(BF16)
            qkv = [_rms_qkv(h, b_ln[j][None], w_qkv, tables, batch=b, seq=s, g=g)
                   for g in range(len(DILATIONS))]
            att = _attention(qkv, seq=s)
            h = _out_proj(att.reshape(b * s, -1), b_w_o[j].astype(BF16), h)
        h = _ffn(h, ffn_ln[i][None], ffn_w_gate[i].astype(BF16), ffn_w_up[i].astype(BF16),
                 ffn_conv_w[i], ffn_conv_b[i][None], ffn_w_down[i].astype(BF16),
                 final_ln[None], seq=s, final_norm=(i == depth - 1))
    return h.reshape(b, s, d)
```

```python
import functools

import jax
import jax.numpy as jnp
from jax import lax
from jax.experimental import pallas as pl
from jax.experimental.pallas import tpu as pltpu

F32 = jnp.float32
BF16 = jnp.bfloat16

EPS = 1e-6
CHUNK = 128
HEAD_DIM = 128
HEADS = 8
DILATIONS = (1, 4, 16)
ATTN_BLOCK = 128
ROT_DIM = 32
ROPE_THETA = 500000.0
ATTN_SCALE = HEAD_DIM ** -0.5
LOG2_E = 1.4426950408889634
MXU_COLS = 256
MAX_ROW_STRIDE = 4
CAST_BLOCK_BYTES = 12 * 1024 * 1024

VMEM_LIMIT_BYTES = 56 * 1024 * 1024


def _params(n_axes):
    return pltpu.CompilerParams(
        dimension_semantics=("arbitrary",) * n_axes,
        vmem_limit_bytes=VMEM_LIMIT_BYTES,
    )


def _rms(x, gain):
    ms = jnp.mean(x * x, axis=-1, keepdims=True)
    return x * lax.rsqrt(ms + EPS) * gain


def _gelu_tanh(x):
    c = 0.7978845608028654
    return 0.5 * x * (1.0 + jnp.tanh(c * (x + 0.044715 * (x * x * x))))


def _rms_gelu_proj_kernel(h_ref, gain_ref, w_ref, o_ref, hn_ref):
    @pl.when(pl.program_id(1) == 0)
    def _():
        hn_ref[...] = _rms(h_ref[...], gain_ref[...]).astype(BF16)

    hn = hn_ref[...]
    for c in range(w_ref.shape[1] // MXU_COLS):
        cols = slice(c * MXU_COLS, (c + 1) * MXU_COLS)
        y = jnp.dot(hn, w_ref[:, cols], preferred_element_type=F32)
        o_ref[:, cols] = _gelu_tanh(y).astype(o_ref.dtype)


def _rms_gelu_proj(h, gain, w, *, tm=1024, tn=2048):
    t, d = h.shape
    n = w.shape[1]
    return pl.pallas_call(
        _rms_gelu_proj_kernel,
        grid=(t // tm, n // tn),
        in_specs=[
            pl.BlockSpec((tm, d), lambda i, j: (i, 0)),
            pl.BlockSpec((1, d), lambda i, j: (0, 0)),
            pl.BlockSpec((d, tn), lambda i, j: (0, j)),
        ],
        out_specs=pl.BlockSpec((tm, tn), lambda i, j: (i, j)),
        out_shape=jax.ShapeDtypeStruct((t, n), BF16),
        scratch_shapes=[pltpu.VMEM((tm, d), BF16)],
        compiler_params=_params(2),
        name="rms_gelu_proj",
    )(h, gain, w)


def _gmlp_mix_kernel(u_ref, v_ref, lng_ref, lnb_ref, ws_ref, bst_ref, wout_ref,
                     h_ref, o_ref, gated_ref, *, tm):
    n_chunks = tm // CHUNK
    groups = ws_ref.shape[0]

    v = v_ref[...].astype(F32)
    mu = jnp.mean(v, axis=-1, keepdims=True)
    vc = v - mu
    var = jnp.mean(vc * vc, axis=-1, keepdims=True)
    vn = (vc * lax.rsqrt(var + EPS) * lng_ref[...] + lnb_ref[...]).astype(BF16)

    row = lax.broadcasted_iota(jnp.int32, (CHUNK, CHUNK), 0)
    col = lax.broadcasted_iota(jnp.int32, (CHUNK, CHUNK), 1)
    causal = col <= row
    for g in range(groups):
        cols = slice(g * CHUNK, (g + 1) * CHUNK)
        rhs = jnp.concatenate(
            [vn[n * CHUNK:(n + 1) * CHUNK, cols] for n in range(n_chunks)], axis=1)
        ws = jnp.where(causal, ws_ref[g], 0.0).astype(BF16)
        mixed = jnp.dot(ws, rhs, preferred_element_type=F32) + bst_ref[:, g:g + 1]
        for n in range(n_chunks):
            rows = slice(n * CHUNK, (n + 1) * CHUNK)
            u = u_ref[rows, cols].astype(F32)
            gated_ref[rows, cols] = (u * mixed[:, n * CHUNK:(n + 1) * CHUNK]).astype(BF16)

    o_ref[...] = h_ref[...] + jnp.dot(gated_ref[...], wout_ref[...],
                                      preferred_element_type=F32)


def _gmlp_mix(z, ln_g, ln_b, w_s, b_s_t, w_out, h, *, tm=512):
    t, d = h.shape
    width = w_out.shape[0]
    groups = w_s.shape[0]
    return pl.pallas_call(
        functools.partial(_gmlp_mix_kernel, tm=tm),
        grid=(t // tm,),
        in_specs=[
            pl.BlockSpec((tm, width), lambda i: (i, 0)),
            pl.BlockSpec((tm, width), lambda i: (i, 1)),
            pl.BlockSpec((1, width), lambda i: (0, 0)),
            pl.BlockSpec((1, width), lambda i: (0, 0)),
            pl.BlockSpec((groups, CHUNK, CHUNK), lambda i: (0, 0, 0)),
            pl.BlockSpec((CHUNK, groups), lambda i: (0, 0)),
            pl.BlockSpec((width, d), lambda i: (0, 0)),
            pl.BlockSpec((tm, d), lambda i: (i, 0)),
        ],
        out_specs=pl.BlockSpec((tm, d), lambda i: (i, 0)),
        out_shape=jax.ShapeDtypeStruct((t, d), F32),
        scratch_shapes=[pltpu.VMEM((tm, width), BF16)],
        compiler_params=_params(1),
        name="gmlp_mix",
    )(z, z, ln_g, ln_b, w_s, b_s_t, w_out, h)


def _rms_qkv_kernel(h_ref, gain_ref, w_ref, cos_ref, sin_ref, o_ref, hn_ref, *scratch, d):
    tm = h_ref.shape[0]
    rows = tm // d
    part = pl.program_id(1)

    @pl.when(part == 0)
    def _():
        hn_ref[...] = _rms(h_ref[...], gain_ref[...]).astype(BF16)

    def project(rotary):
        hn = hn_ref[...]
        for c in range(w_ref.shape[1] // MXU_COLS):
            y = jnp.dot(hn, w_ref[:, c * MXU_COLS:(c + 1) * MXU_COLS],
                        preferred_element_type=F32)
            for hh in range(MXU_COLS // HEAD_DIM):
                head = c * (MXU_COLS // HEAD_DIM) + hh
                lanes = slice(head * HEAD_DIM, (head + 1) * HEAD_DIM)
                t = y[:, hh * HEAD_DIM:(hh + 1) * HEAD_DIM]
                if rotary:
                    t = t * cos_ref[...] + pltpu.roll(t, HEAD_DIM // 2, 1) * sin_ref[...]
                if d == 1:
                    o_ref[0, :, lanes] = t.astype(o_ref.dtype)
                    continue
                y_sc = scratch[0]
                y_sc[head] = t
                d_lo = min(d, MAX_ROW_STRIDE)
                d_hi = d // d_lo
                if d_hi > 1:
                    y2_sc = scratch[1]
                    for r_lo in range(d_lo):
                        y2_sc[head, r_lo * (tm // d_lo):(r_lo + 1) * (tm // d_lo), :] = (
                            y_sc[head, pl.ds(r_lo, tm // d_lo, stride=d_lo), :])
                    y_sc = y2_sc
                for r_lo in range(d_lo):
                    for r_hi in range(d_hi):
                        src = pl.ds(r_lo * (tm // d_lo) + r_hi, rows, stride=d_hi) if d_hi > 1 \
                            else pl.ds(r_lo, rows, stride=d_lo)
                        o_ref[r_hi * d_lo + r_lo, :, lanes] = y_sc[head, src, :].astype(
                            o_ref.dtype)

    @pl.when(part < 2)
    def _():
        project(True)

    @pl.when(part == 2)
    def _():
        project(False)


def _rms_qkv(h, gain, w_qkv, tables, *, batch, seq, g, tm=1024):
    d = DILATIONS[g]
    t, d_model = h.shape
    part = HEADS * HEAD_DIM
    n_dil = len(DILATIONS)
    tiles_per_seq = seq // tm
    table_spec = pl.BlockSpec((tm, HEAD_DIM), lambda i, j: (i % tiles_per_seq, 0))
    return pl.pallas_call(
        functools.partial(_rms_qkv_kernel, d=d),
        grid=(t // tm, 3),
        in_specs=[
            pl.BlockSpec((tm, d_model), lambda i, j: (i, 0)),
            pl.BlockSpec((1, d_model), lambda i, j: (0, 0)),
            pl.BlockSpec((d_model, part), lambda i, j: (0, j * n_dil + g)),
            table_spec, table_spec,
        ],
        out_specs=pl.BlockSpec((None, d, tm // d, part),
                               lambda i, j: (i // tiles_per_seq, 0, i % tiles_per_seq, j)),
        out_shape=jax.ShapeDtypeStruct((batch, d, seq // d, 3 * part), BF16),
        scratch_shapes=[pltpu.VMEM((tm, d_model), BF16)]
        + [pltpu.VMEM((HEADS, tm, HEAD_DIM), F32)] * ((d > 1) + (d > MAX_ROW_STRIDE)),
        compiler_params=_params(2),
        name=f"rms_qkv_d{d}",
    )(h, gain, w_qkv, *tables)


def _attn_block(q_ref, k_ref, v_ref, bias_ref, acc_sc, m_sc, den_sc, d, r, n):
    w = ATTN_BLOCK
    q = q_ref[r, n * w:(n + 1) * w, :]
    first = max(n - 1, 0)
    k = k_ref[r, first * w:(n + 1) * w, :]
    v = v_ref[r, first * w:(n + 1) * w, :]
    nk = k.shape[0]
    s = lax.dot_general(q, k, (((1,), (1,)), ((), ())), preferred_element_type=F32)
    s = s + bias_ref[:, 2 * w - nk:]
    m = jnp.max(s, axis=-1, keepdims=True)
    p = jnp.exp2((s - m) * (ATTN_SCALE * LOG2_E)).astype(BF16)
    v_ones = jnp.concatenate([v, jnp.ones((nk, HEAD_DIM), BF16)], axis=1)
    pv = jnp.dot(p, v_ones, preferred_element_type=F32)
    out_rows = pl.ds(n * w, w) if d == 1 else pl.ds(r + d * w * n, w, stride=d)
    acc_sc[out_rows, :] = pv[:, :HEAD_DIM]
    den_sc[out_rows, :] = pv[:, HEAD_DIM:]
    m_sc[out_rows, :] = jnp.broadcast_to(m, (w, HEAD_DIM))


def _attn_kernel(*refs, seq):
    n_dil = len(DILATIONS)
    qkv_refs = refs[:3 * n_dil]
    o_ref = refs[3 * n_dil]
    bias_ref = refs[3 * n_dil + 1]
    stats = refs[3 * n_dil + 2:]
    w = ATTN_BLOCK

    qi = lax.broadcasted_iota(jnp.int32, (w, 2 * w), 0)
    kj = lax.broadcasted_iota(jnp.int32, (w, 2 * w), 1)
    bias_ref[...] = jnp.where((kj >= qi) & (kj <= qi + w), 0.0, -jnp.inf)

    for g, d in enumerate(DILATIONS):
        q_ref, k_ref, v_ref = qkv_refs[3 * g:3 * g + 3]
        acc_sc, m_sc, den_sc = stats[3 * g:3 * g + 3]
        for r in range(d):
            for n in range(seq // d // w):
                _attn_block(q_ref, k_ref, v_ref, bias_ref, acc_sc, m_sc, den_sc, d, r, n)

    merge_rows = 256

    def merge_body(c, carry):
        rows = pl.ds(pl.multiple_of(c * merge_rows, merge_rows), merge_rows)
        ms = [stats[3 * g + 1][rows, :] for g in range(n_dil)]
        m_all = functools.reduce(jnp.maximum, ms)
        num = jnp.zeros((merge_rows, HEAD_DIM), F32)
        den = jnp.zeros((merge_rows, HEAD_DIM), F32)
        for g in range(n_dil):
            wgt = jnp.exp2((ms[g] - m_all) * (ATTN_SCALE * LOG2_E))
            num = num + wgt * stats[3 * g][rows, :]
            den = den + wgt * stats[3 * g + 2][rows, :]
        o_ref[rows, :] = (num / den).astype(o_ref.dtype)
        return carry

    lax.fori_loop(0, seq // merge_rows, merge_body, 0)


def _attention(qkv_groups, *, seq):
    batch = qkv_groups[0].shape[0]
    n_dil = len(DILATIONS)
    in_specs = []
    for g, d in enumerate(DILATIONS):
        for part in range(3):
            in_specs.append(pl.BlockSpec(
                (None, d, seq // d, HEAD_DIM),
                lambda bi, hi, part=part: (bi, 0, 0, part * HEADS + hi)))
    operands = [qkv_groups[g] for g in range(n_dil) for _ in range(3)]
    return pl.pallas_call(
        functools.partial(_attn_kernel, seq=seq),
        grid=(batch, HEADS),
        in_specs=in_specs,
        out_specs=pl.BlockSpec((None, seq, HEAD_DIM), lambda bi, hi: (bi, 0, hi)),
        out_shape=jax.ShapeDtypeStruct((batch, seq, HEADS * HEAD_DIM), BF16),
        scratch_shapes=[pltpu.VMEM((ATTN_BLOCK, 2 * ATTN_BLOCK), F32)]
        + [pltpu.VMEM((seq, HEAD_DIM), F32)] * (3 * n_dil),
        compiler_params=_params(2),
        name="dilated_attn",
    )(*operands)


def _out_proj_kernel(a_ref, w_ref, h_ref, o_ref):
    o_ref[...] = h_ref[...] + jnp.dot(a_ref[...], w_ref[...], preferred_element_type=F32)


def _out_proj(a, w, h, *, tm=512):
    t, d = h.shape
    k = a.shape[1]
    return pl.pallas_call(
        _out_proj_kernel,
        grid=(t // tm,),
        in_specs=[
            pl.BlockSpec((tm, k), lambda i: (i, 0)),
            pl.BlockSpec((k, d), lambda i: (0, 0)),
            pl.BlockSpec((tm, d), lambda i: (i, 0)),
        ],
        out_specs=pl.BlockSpec((tm, d), lambda i: (i, 0)),
        out_shape=jax.ShapeDtypeStruct((t, d), F32),
        compiler_params=_params(1),
        name="attn_out_proj",
    )(a, w, h)


def _ffn_kernel(h_ref, gain_ref, wg_ref, wu_ref, cw_ref, cb_ref, wd_ref, fin_ref,
                o_ref, hn_ref, gbuf_ref, halo_ref, *, tm, seq, final_norm):
    i = pl.program_id(0)
    j = pl.program_id(1)

    @pl.when(j == 0)
    def _():
        x = h_ref[...]
        hn_ref[...] = _rms(x, gain_ref[...]).astype(BF16)
        o_ref[...] = x

    hn = hn_ref[...]
    g = jnp.dot(hn, wg_ref[...], preferred_element_type=F32)
    u = jnp.dot(hn, wu_ref[...], preferred_element_type=F32)

    seq_start = (i * tm) % seq == 0
    gbuf_ref[0:8, :] = jnp.where(seq_start, 0.0, halo_ref[j])
    gbuf_ref[8:8 + tm, :] = g
    conv = (cw_ref[2:3, :] * g
            + cw_ref[1:2, :] * gbuf_ref[7:7 + tm, :]
            + cw_ref[0:1, :] * gbuf_ref[6:6 + tm, :]
            + cb_ref[...])
    act = (conv * (1.0 / (1.0 + jnp.exp(-conv))) * u).astype(BF16)
    o_ref[...] += jnp.dot(act, wd_ref[...], preferred_element_type=F32)
    halo_ref[j] = gbuf_ref[tm:tm + 8, :]

    if final_norm:
        @pl.when(j == pl.num_programs(1) - 1)
        def _():
            o_ref[...] = _rms(o_ref[...], fin_ref[...])


def _ffn(h, gain, w_gate, w_up, conv_w, conv_b, w_down, final_gain, *, seq,
         final_norm, tm=512, tf=512):
    t, d = h.shape
    f = w_gate.shape[1]
    return pl.pallas_call(
        functools.partial(_ffn_kernel, tm=tm, seq=seq, final_norm=final_norm),
        grid=(t // tm, f // tf),
        in_specs=[
            pl.BlockSpec((tm, d), lambda i, j: (i, 0)),
            pl.BlockSpec((1, d), lambda i, j: (0, 0)),
            pl.BlockSpec((d, tf), lambda i, j: (0, j)),
            pl.BlockSpec((d, tf), lambda i, j: (0, j)),
            pl.BlockSpec((conv_w.shape[0], tf), lambda i, j: (0, j)),
            pl.BlockSpec((1, tf), lambda i, j: (0, j)),
            pl.BlockSpec((tf, d), lambda i, j: (j, 0)),
            pl.BlockSpec((1, d), lambda i, j: (0, 0)),
        ],
        out_specs=pl.BlockSpec((tm, d), lambda i, j: (i, 0)),
        out_shape=jax.ShapeDtypeStruct((t, d), F32),
        scratch_shapes=[
            pltpu.VMEM((tm, d), BF16),
            pltpu.VMEM((tm + 8, tf), F32),
            pltpu.VMEM((f // tf, 8, tf), F32),
        ],
        compiler_params=_params(2),
        name="conv_ffn",
    )(h, gain, w_gate, w_up, conv_w, conv_b, w_down, final_gain)


def _rope_lane_order(x):
    half = ROT_DIM // 2
    lane = lax.broadcasted_iota(jnp.int32, x.shape, 1)
    up = pltpu.roll(x, HEAD_DIM - half, 1)
    down = pltpu.roll(x, HEAD_DIM // 2 - half, 1)
    return jnp.where(lane < half, x,
                     jnp.where(lane < HEAD_DIM // 2, up,
                               jnp.where(lane < HEAD_DIM // 2 + half, down, x)))


def _cast_kernel(w_ref, o_ref, *, rope_cols):
    cols = w_ref.shape[1]
    for c in range(rope_cols // HEAD_DIM):
        lanes = slice(c * HEAD_DIM, (c + 1) * HEAD_DIM)
        o_ref[:, lanes] = _rope_lane_order(w_ref[:, lanes]).astype(o_ref.dtype)
    if rope_cols < cols:
        o_ref[:, rope_cols:] = w_ref[:, rope_cols:].astype(o_ref.dtype)


def _cast_bf16(w_stack, layer, *, rope_cols=0):
    _, rows, cols = w_stack.shape
    tr = 512 if 512 * cols * 4 <= CAST_BLOCK_BYTES else 256
    return pl.pallas_call(
        functools.partial(_cast_kernel, rope_cols=rope_cols),
        grid=(rows // tr,),
        in_specs=[pl.BlockSpec((None, tr, cols), lambda i: (layer, i, 0))],
        out_specs=pl.BlockSpec((tr, cols), lambda i: (i, 0)),
        out_shape=jax.ShapeDtypeStruct((rows, cols), BF16),
        compiler_params=_params(1),
        name="cast_bf16_rope" if rope_cols else "cast_bf16",
    )(w_stack)


def _rope_tables(seq):
    half = ROT_DIM // 2
    pos = jnp.arange(seq, dtype=F32)
    inv = jnp.float32(ROPE_THETA) ** (-jnp.arange(0, ROT_DIM, 2, dtype=F32) / ROT_DIM)
    ang = pos[:, None] * inv[None, :]
    cos, sin = jnp.cos(ang), jnp.sin(ang)
    gap = jnp.zeros((seq, HEAD_DIM // 2 - half), F32)
    cos_t = jnp.concatenate([cos, gap + 1.0, cos, gap + 1.0], axis=1)
    sin_t = jnp.concatenate([-sin, gap, sin, gap], axis=1)
    return cos_t, sin_t


def kernel(x, a_ln, a_w_in, a_sgu_ln_g, a_sgu_ln_b, a_w_s, a_b_s, a_w_out, b_ln, b_w_qkv, b_w_o, ffn_ln, ffn_w_gate, ffn_w_up, ffn_conv_w, ffn_conv_b, ffn_w_down, final_ln):
    b, s, d = x.shape
    depth = ffn_ln.shape[0]
    h = x.reshape(b * s, d)
    tables = _rope_tables(s)
    for i in range(depth):
        j = i // 2
        if i % 2 == 0:
            z = _rms_gelu_proj(h, a_ln[j][None], _cast_bf16(a_w_in, j))
            h = _gmlp_mix(z, a_sgu_ln_g[j][None], a_sgu_ln_b[j][None], a_w_s[j],
                          a_b_s[j].T, _cast_bf16(a_w_out, j), h)
        else:
            qk_cols = 2 * len(DILATIONS) * HEADS * HEAD_DIM
            w_qkv = _cast_bf16(b_w_qkv, j, rope_cols=qk_cols)
            qkv = [_rms_qkv(h, b_ln[j][None], w_qkv, tables, batch=b, seq=s, g=g)
                   for g in range(len(DILATIONS))]
            att = _attention(qkv, seq=s)
            h = _out_proj(att.reshape(b * s, -1), _cast_bf16(b_w_o, j), h)
        h = _ffn(h, ffn_ln[i][None], _cast_bf16(ffn_w_gate, i), _cast_bf16(ffn_w_up, i),
                 ffn_conv_w[i], ffn_conv_b[i][None], _cast_bf16(ffn_w_down, i),
                 final_ln[None], seq=s, final_norm=(i == depth - 1))
    return h.reshape(b, s, d)
```

```python
import functools

import jax
import jax.numpy as jnp
from jax import lax
from jax.experimental import pallas as pl
from jax.experimental.pallas import tpu as pltpu

F32 = jnp.float32
BF16 = jnp.bfloat16

EPS = 1e-6
CHUNK = 128
HEAD_DIM = 128
HEADS = 8
DILATIONS = (1, 4, 16)
ATTN_BLOCK = 128
ROT_DIM = 32
ROPE_THETA = 500000.0
ATTN_SCALE = HEAD_DIM ** -0.5
LOG2_E = 1.4426950408889634
MXU_COLS = 256
MAX_ROW_STRIDE = 4
CAST_BLOCK_BYTES = 12 * 1024 * 1024

VMEM_LIMIT_BYTES = 56 * 1024 * 1024


def _params(n_axes):
    return pltpu.CompilerParams(
        dimension_semantics=("arbitrary",) * n_axes,
        vmem_limit_bytes=VMEM_LIMIT_BYTES,
    )


def _rms(x, gain):
    ms = jnp.mean(x * x, axis=-1, keepdims=True)
    return x * lax.rsqrt(ms + EPS) * gain


def _gelu_tanh(x):
    c = 0.7978845608028654
    return 0.5 * x * (1.0 + jnp.tanh(c * (x + 0.044715 * (x * x * x))))


def _rms_gelu_proj_kernel(h_ref, gain_ref, w_ref, o_ref, hn_ref):
    @pl.when(pl.program_id(1) == 0)
    def _():
        hn_ref[...] = _rms(h_ref[...], gain_ref[...]).astype(BF16)

    hn = hn_ref[...]
    for c in range(w_ref.shape[1] // MXU_COLS):
        cols = slice(c * MXU_COLS, (c + 1) * MXU_COLS)
        y = jnp.dot(hn, w_ref[:, cols], preferred_element_type=F32)
        o_ref[:, cols] = _gelu_tanh(y).astype(o_ref.dtype)


def _rms_gelu_proj(h, gain, w, *, tm=1024, tn=2048):
    t, d = h.shape
    n = w.shape[1]
    return pl.pallas_call(
        _rms_gelu_proj_kernel,
        grid=(t // tm, n // tn),
        in_specs=[
            pl.BlockSpec((tm, d), lambda i, j: (i, 0)),
            pl.BlockSpec((1, d), lambda i, j: (0, 0)),
            pl.BlockSpec((d, tn), lambda i, j: (0, j)),
        ],
        out_specs=pl.BlockSpec((tm, tn), lambda i, j: (i, j)),
        out_shape=jax.ShapeDtypeStruct((t, n), BF16),
        scratch_shapes=[pltpu.VMEM((tm, d), BF16)],
        compiler_params=_params(2),
        name="rms_gelu_proj",
    )(h, gain, w)


def _gmlp_mix_kernel(u_ref, v_ref, lng_ref, lnb_ref, ws_ref, bst_ref, wout_ref,
                     h_ref, o_ref, gated_ref, *, tm):
    n_chunks = tm // CHUNK
    groups = ws_ref.shape[0]

    v = v_ref[...].astype(F32)
    mu = jnp.mean(v, axis=-1, keepdims=True)
    vc = v - mu
    var = jnp.mean(vc * vc, axis=-1, keepdims=True)
    vn = (vc * lax.rsqrt(var + EPS) * lng_ref[...] + lnb_ref[...]).astype(BF16)

    row = lax.broadcasted_iota(jnp.int32, (CHUNK, CHUNK), 0)
    col = lax.broadcasted_iota(jnp.int32, (CHUNK, CHUNK), 1)
    causal = col <= row
    for g in range(groups):
        cols = slice(g * CHUNK, (g + 1) * CHUNK)
        rhs = jnp.concatenate(
            [vn[n * CHUNK:(n + 1) * CHUNK, cols] for n in range(n_chunks)], axis=1)
        ws = jnp.where(causal, ws_ref[g], 0.0).astype(BF16)
        mixed = jnp.dot(ws, rhs, preferred_element_type=F32) + bst_ref[:, g:g + 1]
        for n in range(n_chunks):
            rows = slice(n * CHUNK, (n + 1) * CHUNK)
            u = u_ref[rows, cols].astype(F32)
            gated_ref[rows, cols] = (u * mixed[:, n * CHUNK:(n + 1) * CHUNK]).astype(BF16)

    o_ref[...] = h_ref[...] + jnp.dot(gated_ref[...], wout_ref[...],
                                      preferred_element_type=F32)


def _gmlp_mix(z, ln_g, ln_b, w_s, b_s_t, w_out, h, *, tm=512):
    t, d = h.shape
    width = w_out.shape[0]
    groups = w_s.shape[0]
    return pl.pallas_call(
        functools.partial(_gmlp_mix_kernel, tm=tm),
        grid=(t // tm,),
        in_specs=[
            pl.BlockSpec((tm, width), lambda i: (i, 0)),
            pl.BlockSpec((tm, width), lambda i: (i, 1)),
            pl.BlockSpec((1, width), lambda i: (0, 0)),
            pl.BlockSpec((1, width), lambda i: (0, 0)),
            pl.BlockSpec((groups, CHUNK, CHUNK), lambda i: (0, 0, 0)),
            pl.BlockSpec((CHUNK, groups), lambda i: (0, 0)),
            pl.BlockSpec((width, d), lambda i: (0, 0)),
            pl.BlockSpec((tm, d), lambda i: (i, 0)),
        ],
        out_specs=pl.BlockSpec((tm, d), lambda i: (i, 0)),
        out_shape=jax.ShapeDtypeStruct((t, d), F32),
        scratch_shapes=[pltpu.VMEM((tm, width), BF16)],
        compiler_params=_params(1),
        name="gmlp_mix",
    )(z, z, ln_g, ln_b, w_s, b_s_t, w_out, h)


def _rms_qkv_kernel(h_ref, gain_ref, w_ref, cos_ref, sin_ref, o_ref, hn_ref, *scratch, d):
    tm = h_ref.shape[0]
    rows = tm // d
    part = pl.program_id(1)

    @pl.when(part == 0)
    def _():
        hn_ref[...] = _rms(h_ref[...], gain_ref[...]).astype(BF16)

    def project(rotary):
        hn = hn_ref[...]
        for c in range(w_ref.shape[1] // MXU_COLS):
            y = jnp.dot(hn, w_ref[:, c * MXU_COLS:(c + 1) * MXU_COLS],
                        preferred_element_type=F32)
            for hh in range(MXU_COLS // HEAD_DIM):
                head = c * (MXU_COLS // HEAD_DIM) + hh
                lanes = slice(head * HEAD_DIM, (head + 1) * HEAD_DIM)
                t = y[:, hh * HEAD_DIM:(hh + 1) * HEAD_DIM]
                if rotary:
                    t = t * cos_ref[...] + pltpu.roll(t, HEAD_DIM // 2, 1) * sin_ref[...]
                if d == 1:
                    o_ref[0, :, lanes] = t.astype(o_ref.dtype)
                    continue
                y_sc = scratch[0]
                y_sc[head] = t
                d_lo = min(d, MAX_ROW_STRIDE)
                d_hi = d // d_lo
                if d_hi > 1:
                    y2_sc = scratch[1]
                    for r_lo in range(d_lo):
                        y2_sc[head, r_lo * (tm // d_lo):(r_lo + 1) * (tm // d_lo), :] = (
                            y_sc[head, pl.ds(r_lo, tm // d_lo, stride=d_lo), :])
                    y_sc = y2_sc
                for r_lo in range(d_lo):
                    for r_hi in range(d_hi):
                        src = pl.ds(r_lo * (tm // d_lo) + r_hi, rows, stride=d_hi) if d_hi > 1 \
                            else pl.ds(r_lo, rows, stride=d_lo)
                        o_ref[r_hi * d_lo + r_lo, :, lanes] = y_sc[head, src, :].astype(
                            o_ref.dtype)

    @pl.when(part < 2)
    def _():
        project(True)

    @pl.when(part == 2)
    def _():
        project(False)


def _rms_qkv(h, gain, w_qkv, tables, *, batch, seq, g, tm=1024):
    d = DILATIONS[g]
    t, d_model = h.shape
    part = HEADS * HEAD_DIM
    n_dil = len(DILATIONS)
    tiles_per_seq = seq // tm
    table_spec = pl.BlockSpec((tm, HEAD_DIM), lambda i, j: (i % tiles_per_seq, 0))
    return pl.pallas_call(
        functools.partial(_rms_qkv_kernel, d=d),
        grid=(t // tm, 3),
        in_specs=[
            pl.BlockSpec((tm, d_model), lambda i, j: (i, 0)),
            pl.BlockSpec((1, d_model), lambda i, j: (0, 0)),
            pl.BlockSpec((d_model, part), lambda i, j: (0, j * n_dil + g)),
            table_spec, table_spec,
        ],
        out_specs=pl.BlockSpec((None, d, tm // d, part),
                               lambda i, j: (i // tiles_per_seq, 0, i % tiles_per_seq, j)),
        out_shape=jax.ShapeDtypeStruct((batch, d, seq // d, 3 * part), BF16),
        scratch_shapes=[pltpu.VMEM((tm, d_model), BF16)]
        + [pltpu.VMEM((HEADS, tm, HEAD_DIM), F32)] * ((d > 1) + (d > MAX_ROW_STRIDE)),
        compiler_params=_params(2),
        name=f"rms_qkv_d{d}",
    )(h, gain, w_qkv, *tables)


def _attn_block(q_ref, k_ref, v_ref, bias_ref, acc_sc, m_sc, den_sc, d, r, n):
    w = ATTN_BLOCK
    q = q_ref[r, n * w:(n + 1) * w, :]
    first = max(n - 1, 0)
    k = k_ref[r, first * w:(n + 1) * w, :]
    v = v_ref[r, first * w:(n + 1) * w, :]
    nk = k.shape[0]
    s = lax.dot_general(q, k, (((1,), (1,)), ((), ())), preferred_element_type=F32)
    s = s + bias_ref[:, 2 * w - nk:]
    m = jnp.max(s, axis=-1, keepdims=True)
    p = jnp.exp2((s - m) * (ATTN_SCALE * LOG2_E)).astype(BF16)
    v_ones = jnp.concatenate([v, jnp.ones((nk, HEAD_DIM), BF16)], axis=1)
    pv = jnp.dot(p, v_ones, preferred_element_type=F32)
    out_rows = pl.ds(n * w, w) if d == 1 else pl.ds(r + d * w * n, w, stride=d)
    acc_sc[out_rows, :] = pv[:, :HEAD_DIM]
    den_sc[out_rows, :] = pv[:, HEAD_DIM:]
    m_sc[out_rows, :] = jnp.broadcast_to(m, (w, HEAD_DIM))


def _attn_kernel(*refs, seq):
    n_dil = len(DILATIONS)
    qkv_refs = refs[:3 * n_dil]
    o_ref = refs[3 * n_dil]
    bias_ref = refs[3 * n_dil + 1]
    stats = refs[3 * n_dil + 2:]
    w = ATTN_BLOCK

    qi = lax.broadcasted_iota(jnp.int32, (w, 2 * w), 0)
    kj = lax.broadcasted_iota(jnp.int32, (w, 2 * w), 1)
    bias_ref[...] = jnp.where((kj >= qi) & (kj <= qi + w), 0.0, -jnp.inf)

    for g, d in enumerate(DILATIONS):
        q_ref, k_ref, v_ref = qkv_refs[3 * g:3 * g + 3]
        acc_sc, m_sc, den_sc = stats[3 * g:3 * g + 3]
        for r in range(d):
            for n in range(seq // d // w):
                _attn_block(q_ref, k_ref, v_ref, bias_ref, acc_sc, m_sc, den_sc, d, r, n)

    merge_rows = 256

    def merge_body(c, carry):
        rows = pl.ds(pl.multiple_of(c * merge_rows, merge_rows), merge_rows)
        ms = [stats[3 * g + 1][rows, :] for g in range(n_dil)]
        m_all = functools.reduce(jnp.maximum, ms)
        num = jnp.zeros((merge_rows, HEAD_DIM), F32)
        den = jnp.zeros((merge_rows, HEAD_DIM), F32)
        for g in range(n_dil):
            wgt = jnp.exp2((ms[g] - m_all) * (ATTN_SCALE * LOG2_E))
            num = num + wgt * stats[3 * g][rows, :]
            den = den + wgt * stats[3 * g + 2][rows, :]
        o_ref[rows, :] = (num / den).astype(o_ref.dtype)
        return carry

    lax.fori_loop(0, seq // merge_rows, merge_body, 0)


def _attention(qkv_groups, *, seq):
    batch = qkv_groups[0].shape[0]
    n_dil = len(DILATIONS)
    in_specs = []
    for g, d in enumerate(DILATIONS):
        for part in range(3):
            in_specs.append(pl.BlockSpec(
                (None, d, seq // d, HEAD_DIM),
                lambda bi, hi, part=part: (bi, 0, 0, part * HEADS + hi)))
    operands = [qkv_groups[g] for g in range(n_dil) for _ in range(3)]
    return pl.pallas_call(
        functools.partial(_attn_kernel, seq=seq),
        grid=(batch, HEADS),
        in_specs=in_specs,
        out_specs=pl.BlockSpec((None, seq, HEAD_DIM), lambda bi, hi: (bi, 0, hi)),
        out_shape=jax.ShapeDtypeStruct((batch, seq, HEADS * HEAD_DIM), BF16),
        scratch_shapes=[pltpu.VMEM((ATTN_BLOCK, 2 * ATTN_BLOCK), F32)]
        + [pltpu.VMEM((seq, HEAD_DIM), F32)] * (3 * n_dil),
        compiler_params=_params(2),
        name="dilated_attn",
    )(*operands)


def _out_proj_kernel(a_ref, w_ref, h_ref, o_ref):
    o_ref[...] = h_ref[...] + jnp.dot(a_ref[...], w_ref[...], preferred_element_type=F32)


def _out_proj(a, w, h, *, tm=512):
    t, d = h.shape
    k = a.shape[1]
    return pl.pallas_call(
        _out_proj_kernel,
        grid=(t // tm,),
        in_specs=[
            pl.BlockSpec((tm, k), lambda i: (i, 0)),
            pl.BlockSpec((k, d), lambda i: (0, 0)),
            pl.BlockSpec((tm, d), lambda i: (i, 0)),
        ],
        out_specs=pl.BlockSpec((tm, d), lambda i: (i, 0)),
        out_shape=jax.ShapeDtypeStruct((t, d), F32),
        compiler_params=_params(1),
        name="attn_out_proj",
    )(a, w, h)


def _ffn_kernel(h_hbm, gain_ref, wg_ref, wu_ref, cw_ref, cb_ref, wd_ref, fin_ref,
                o_ref, h_buf, h_sem, hn_ref, gbuf_ref, halo_ref, *, tm, seq, final_norm):
    i = pl.program_id(0)
    j = pl.program_id(1)

    def h_copy(tile):
        return pltpu.make_async_copy(h_hbm.at[pl.ds(tile * tm, tm), :], h_buf, h_sem)

    @pl.when((i == 0) & (j == 0))
    def _():
        h_copy(0).start()

    @pl.when(j == 0)
    def _():
        h_copy(i).wait()
        x = h_buf[...]
        hn_ref[...] = _rms(x, gain_ref[...]).astype(BF16)
        o_ref[...] = x

    @pl.when((j == 1) & (i + 1 < pl.num_programs(0)))
    def _():
        h_copy(i + 1).start()

    hn = hn_ref[...]
    g = jnp.dot(hn, wg_ref[...], preferred_element_type=F32)
    u = jnp.dot(hn, wu_ref[...], preferred_element_type=F32)

    seq_start = (i * tm) % seq == 0
    gbuf_ref[0:8, :] = jnp.where(seq_start, 0.0, halo_ref[j])
    gbuf_ref[8:8 + tm, :] = g
    conv = (cw_ref[2:3, :] * g
            + cw_ref[1:2, :] * gbuf_ref[7:7 + tm, :]
            + cw_ref[0:1, :] * gbuf_ref[6:6 + tm, :]
            + cb_ref[...])
    act = (conv * (1.0 / (1.0 + jnp.exp(-conv))) * u).astype(BF16)
    o_ref[...] += jnp.dot(act, wd_ref[...], preferred_element_type=F32)
    halo_ref[j] = gbuf_ref[tm:tm + 8, :]

    if final_norm:
        @pl.when(j == pl.num_programs(1) - 1)
        def _():
            o_ref[...] = _rms(o_ref[...], fin_ref[...])


def _ffn(h, gain, w_gate, w_up, conv_w, conv_b, w_down, final_gain, *, seq,
         final_norm, tm=1024, tf=512):
    t, d = h.shape
    f = w_gate.shape[1]
    return pl.pallas_call(
        functools.partial(_ffn_kernel, tm=tm, seq=seq, final_norm=final_norm),
        grid=(t // tm, f // tf),
        in_specs=[
            pl.BlockSpec(memory_space=pl.ANY),
            pl.BlockSpec((1, d), lambda i, j: (0, 0)),
            pl.BlockSpec((d, tf), lambda i, j: (0, j)),
            pl.BlockSpec((d, tf), lambda i, j: (0, j)),
            pl.BlockSpec((conv_w.shape[0], tf), lambda i, j: (0, j)),
            pl.BlockSpec((1, tf), lambda i, j: (0, j)),
            pl.BlockSpec((tf, d), lambda i, j: (j, 0)),
            pl.BlockSpec((1, d), lambda i, j: (0, 0)),
        ],
        out_specs=pl.BlockSpec((tm, d), lambda i, j: (i, 0)),
        out_shape=jax.ShapeDtypeStruct((t, d), F32),
        scratch_shapes=[
            pltpu.VMEM((tm, d), F32),
            pltpu.SemaphoreType.DMA(()),
            pltpu.VMEM((tm, d), BF16),
            pltpu.VMEM((tm + 8, tf), F32),
            pltpu.VMEM((f // tf, 8, tf), F32),
        ],
        compiler_params=_params(2),
        name="conv_ffn",
    )(h, gain, w_gate, w_up, conv_w, conv_b, w_down, final_gain)


def _rope_lane_order(x):
    half = ROT_DIM // 2
    lane = lax.broadcasted_iota(jnp.int32, x.shape, 1)
    up = pltpu.roll(x, HEAD_DIM - half, 1)
    down = pltpu.roll(x, HEAD_DIM // 2 - half, 1)
    return jnp.where(lane < half, x,
                     jnp.where(lane < HEAD_DIM // 2, up,
                               jnp.where(lane < HEAD_DIM // 2 + half, down, x)))


def _cast_kernel(w_ref, o_ref, *, rope_cols):
    cols = w_ref.shape[1]
    for c in range(rope_cols // HEAD_DIM):
        lanes = slice(c * HEAD_DIM, (c + 1) * HEAD_DIM)
        o_ref[:, lanes] = _rope_lane_order(w_ref[:, lanes]).astype(o_ref.dtype)
    if rope_cols < cols:
        o_ref[:, rope_cols:] = w_ref[:, rope_cols:].astype(o_ref.dtype)


def _cast_bf16(w_stack, layer, *, rope_cols=0):
    _, rows, cols = w_stack.shape
    tr = 512 if 512 * cols * 4 <= CAST_BLOCK_BYTES else 256
    return pl.pallas_call(
        functools.partial(_cast_kernel, rope_cols=rope_cols),
        grid=(rows // tr,),
        in_specs=[pl.BlockSpec((None, tr, cols), lambda i: (layer, i, 0))],
        out_specs=pl.BlockSpec((tr, cols), lambda i: (i, 0)),
        out_shape=jax.ShapeDtypeStruct((rows, cols), BF16),
        compiler_params=_params(1),
        name="cast_bf16_rope" if rope_cols else "cast_bf16",
    )(w_stack)


def _rope_tables(seq):
    half = ROT_DIM // 2
    pos = jnp.arange(seq, dtype=F32)
    inv = jnp.float32(ROPE_THETA) ** (-jnp.arange(0, ROT_DIM, 2, dtype=F32) / ROT_DIM)
    ang = pos[:, None] * inv[None, :]
    cos, sin = jnp.cos(ang), jnp.sin(ang)
    gap = jnp.zeros((seq, HEAD_DIM // 2 - half), F32)
    cos_t = jnp.concatenate([cos, gap + 1.0, cos, gap + 1.0], axis=1)
    sin_t = jnp.concatenate([-sin, gap, sin, gap], axis=1)
    return cos_t, sin_t


def kernel(x, a_ln, a_w_in, a_sgu_ln_g, a_sgu_ln_b, a_w_s, a_b_s, a_w_out, b_ln, b_w_qkv, b_w_o, ffn_ln, ffn_w_gate, ffn_w_up, ffn_conv_w, ffn_conv_b, ffn_w_down, final_ln):
    b, s, d = x.shape
    depth = ffn_ln.shape[0]
    h = x.reshape(b * s, d)
    tables = _rope_tables(s)
    for i in range(depth):
        j = i // 2
        if i % 2 == 0:
            z = _rms_gelu_proj(h, a_ln[j][None], _cast_bf16(a_w_in, j))
            h = _gmlp_mix(z, a_sgu_ln_g[j][None], a_sgu_ln_b[j][None], a_w_s[j],
                          a_b_s[j].T, _cast_bf16(a_w_out, j), h)
        else:
            qk_cols = 2 * len(DILATIONS) * HEADS * HEAD_DIM
            w_qkv = _cast_bf16(b_w_qkv, j, rope_cols=qk_cols)
            qkv = [_rms_qkv(h, b_ln[j][None], w_qkv, tables, batch=b, seq=s, g=g)
                   for g in range(len(DILATIONS))]
            att = _attention(qkv, seq=s)
            h = _out_proj(att.reshape(b * s, -1), _cast_bf16(b_w_o, j), h)
        h = _ffn(h, ffn_ln[i][None], _cast_bf16(ffn_w_gate, i), _cast_bf16(ffn_w_up, i),
                 ffn_conv_w[i], ffn_conv_b[i][None], _cast_bf16(ffn_w_down, i),
                 final_ln[None], seq=s, final_norm=(i == depth - 1))
    return h.reshape(b, s, d)
```

```python
import functools

import jax
import jax.numpy as jnp
from jax import lax
from jax.experimental import pallas as pl
from jax.experimental.pallas import tpu as pltpu

F32 = jnp.float32
BF16 = jnp.bfloat16

EPS = 1e-6
CHUNK = 128
HEAD_DIM = 128
HEADS = 8
DILATIONS = (1, 4, 16)
ATTN_BLOCK = 128
ROT_DIM = 32
ROPE_THETA = 500000.0
ATTN_SCALE = HEAD_DIM ** -0.5
LOG2_E = 1.4426950408889634
MXU_COLS = 256
MAX_ROW_STRIDE = 4
CAST_BLOCK_BYTES = 12 * 1024 * 1024

VMEM_LIMIT_BYTES = 56 * 1024 * 1024


def _params(n_axes):
    return pltpu.CompilerParams(
        dimension_semantics=("arbitrary",) * n_axes,
        vmem_limit_bytes=VMEM_LIMIT_BYTES,
    )


def _rms(x, gain):
    ms = jnp.mean(x * x, axis=-1, keepdims=True)
    return x * lax.rsqrt(ms + EPS) * gain


def _gelu_tanh(x):
    c = 0.7978845608028654
    return 0.5 * x * (1.0 + jnp.tanh(c * (x + 0.044715 * (x * x * x))))


def _rms_gelu_proj_kernel(h_ref, gain_ref, w_ref, o_ref, hn_ref):
    @pl.when(pl.program_id(1) == 0)
    def _():
        hn_ref[...] = _rms(h_ref[...], gain_ref[...]).astype(BF16)

    hn = hn_ref[...]
    for c in range(w_ref.shape[1] // MXU_COLS):
        cols = slice(c * MXU_COLS, (c + 1) * MXU_COLS)
        y = jnp.dot(hn, w_ref[:, cols], preferred_element_type=F32)
        o_ref[:, cols] = _gelu_tanh(y).astype(o_ref.dtype)


def _rms_gelu_proj(h, gain, w, *, tm=1024, tn=2048):
    t, d = h.shape
    n = w.shape[1]
    return pl.pallas_call(
        _rms_gelu_proj_kernel,
        grid=(t // tm, n // tn),
        in_specs=[
            pl.BlockSpec((tm, d), lambda i, j: (i, 0)),
            pl.BlockSpec((1, d), lambda i, j: (0, 0)),
            pl.BlockSpec((d, tn), lambda i, j: (0, j)),
        ],
        out_specs=pl.BlockSpec((tm, tn), lambda i, j: (i, j)),
        out_shape=jax.ShapeDtypeStruct((t, n), BF16),
        scratch_shapes=[pltpu.VMEM((tm, d), BF16)],
        compiler_params=_params(2),
        name="rms_gelu_proj",
    )(h, gain, w)


def _gmlp_mix_kernel(u_ref, v_ref, lng_ref, lnb_ref, ws_ref, bst_ref, wout_ref,
                     h_ref, o_ref, gated_ref, *, tm):
    n_chunks = tm // CHUNK
    groups = ws_ref.shape[0]

    v = v_ref[...].astype(F32)
    mu = jnp.mean(v, axis=-1, keepdims=True)
    vc = v - mu
    var = jnp.mean(vc * vc, axis=-1, keepdims=True)
    vn = (vc * lax.rsqrt(var + EPS) * lng_ref[...] + lnb_ref[...]).astype(BF16)

    row = lax.broadcasted_iota(jnp.int32, (CHUNK, CHUNK), 0)
    col = lax.broadcasted_iota(jnp.int32, (CHUNK, CHUNK), 1)
    causal = col <= row
    for g in range(groups):
        cols = slice(g * CHUNK, (g + 1) * CHUNK)
        rhs = jnp.concatenate(
            [vn[n * CHUNK:(n + 1) * CHUNK, cols] for n in range(n_chunks)], axis=1)
        ws = jnp.where(causal, ws_ref[g], 0.0).astype(BF16)
        mixed = jnp.dot(ws, rhs, preferred_element_type=F32) + bst_ref[:, g:g + 1]
        for n in range(n_chunks):
            rows = slice(n * CHUNK, (n + 1) * CHUNK)
            u = u_ref[rows, cols].astype(F32)
            gated_ref[rows, cols] = (u * mixed[:, n * CHUNK:(n + 1) * CHUNK]).astype(BF16)

    o_ref[...] = h_ref[...] + jnp.dot(gated_ref[...], wout_ref[...],
                                      preferred_element_type=F32)


def _gmlp_mix(z, ln_g, ln_b, w_s, b_s_t, w_out, h, *, tm=512):
    t, d = h.shape
    width = w_out.shape[0]
    groups = w_s.shape[0]
    return pl.pallas_call(
        functools.partial(_gmlp_mix_kernel, tm=tm),
        grid=(t // tm,),
        in_specs=[
            pl.BlockSpec((tm, width), lambda i: (i, 0)),
            pl.BlockSpec((tm, width), lambda i: (i, 1)),
            pl.BlockSpec((1, width), lambda i: (0, 0)),
            pl.BlockSpec((1, width), lambda i: (0, 0)),
            pl.BlockSpec((groups, CHUNK, CHUNK), lambda i: (0, 0, 0)),
            pl.BlockSpec((CHUNK, groups), lambda i: (0, 0)),
            pl.BlockSpec((width, d), lambda i: (0, 0)),
            pl.BlockSpec((tm, d), lambda i: (i, 0)),
        ],
        out_specs=pl.BlockSpec((tm, d), lambda i: (i, 0)),
        out_shape=jax.ShapeDtypeStruct((t, d), F32),
        scratch_shapes=[pltpu.VMEM((tm, width), BF16)],
        compiler_params=_params(1),
        name="gmlp_mix",
    )(z, z, ln_g, ln_b, w_s, b_s_t, w_out, h)


def _rms_qkv_kernel(h_ref, gain_ref, w_ref, cos_ref, sin_ref, o_ref, hn_ref, *scratch, d):
    tm = h_ref.shape[0]
    rows = tm // d
    part = pl.program_id(1)

    @pl.when(part == 0)
    def _():
        hn_ref[...] = _rms(h_ref[...], gain_ref[...]).astype(BF16)

    def project(rotary):
        hn = hn_ref[...]
        for c in range(w_ref.shape[1] // MXU_COLS):
            y = jnp.dot(hn, w_ref[:, c * MXU_COLS:(c + 1) * MXU_COLS],
                        preferred_element_type=F32)
            for hh in range(MXU_COLS // HEAD_DIM):
                head = c * (MXU_COLS // HEAD_DIM) + hh
                lanes = slice(head * HEAD_DIM, (head + 1) * HEAD_DIM)
                t = y[:, hh * HEAD_DIM:(hh + 1) * HEAD_DIM]
                if rotary:
                    t = t * cos_ref[...] + pltpu.roll(t, HEAD_DIM // 2, 1) * sin_ref[...]
                if d == 1:
                    o_ref[0, :, lanes] = t.astype(o_ref.dtype)
                    continue
                y_sc = scratch[0]
                y_sc[head] = t
                d_lo = min(d, MAX_ROW_STRIDE)
                d_hi = d // d_lo
                if d_hi > 1:
                    y2_sc = scratch[1]
                    for r_lo in range(d_lo):
                        y2_sc[head, r_lo * (tm // d_lo):(r_lo + 1) * (tm // d_lo), :] = (
                            y_sc[head, pl.ds(r_lo, tm // d_lo, stride=d_lo), :])
                    y_sc = y2_sc
                for r_lo in range(d_lo):
                    for r_hi in range(d_hi):
                        src = pl.ds(r_lo * (tm // d_lo) + r_hi, rows, stride=d_hi) if d_hi > 1 \
                            else pl.ds(r_lo, rows, stride=d_lo)
                        o_ref[r_hi * d_lo + r_lo, :, lanes] = y_sc[head, src, :].astype(
                            o_ref.dtype)

    @pl.when(part < 2)
    def _():
        project(True)

    @pl.when(part == 2)
    def _():
        project(False)


def _rms_qkv(h, gain, w_qkv, tables, *, batch, seq, g, tm=1024):
    d = DILATIONS[g]
    t, d_model = h.shape
    part = HEADS * HEAD_DIM
    n_dil = len(DILATIONS)
    tiles_per_seq = seq // tm
    table_spec = pl.BlockSpec((tm, HEAD_DIM), lambda i, j: (i % tiles_per_seq, 0))
    return pl.pallas_call(
        functools.partial(_rms_qkv_kernel, d=d),
        grid=(t // tm, 3),
        in_specs=[
            pl.BlockSpec((tm, d_model), lambda i, j: (i, 0)),
            pl.BlockSpec((1, d_model), lambda i, j: (0, 0)),
            pl.BlockSpec((d_model, part), lambda i, j: (0, j * n_dil + g)),
            table_spec, table_spec,
        ],
        out_specs=pl.BlockSpec((None, d, tm // d, part),
                               lambda i, j: (i // tiles_per_seq, 0, i % tiles_per_seq, j)),
        out_shape=jax.ShapeDtypeStruct((batch, d, seq // d, 3 * part), BF16),
        scratch_shapes=[pltpu.VMEM((tm, d_model), BF16)]
        + [pltpu.VMEM((HEADS, tm, HEAD_DIM), F32)] * ((d > 1) + (d > MAX_ROW_STRIDE)),
        compiler_params=_params(2),
        name=f"rms_qkv_d{d}",
    )(h, gain, w_qkv, *tables)


def _stage_split(d):
    d_lo = min(d, MAX_ROW_STRIDE)
    return d_lo, d // d_lo


def _attn_block(q_ref, k_ref, v_ref, bias_ref, o_sc, lse_sc, d, r, n, seq):
    w = ATTN_BLOCK
    q = q_ref[r, n * w:(n + 1) * w, :]
    first = max(n - 1, 0)
    k = k_ref[r, first * w:(n + 1) * w, :]
    v = v_ref[r, first * w:(n + 1) * w, :]
    nk = k.shape[0]
    s = lax.dot_general(q, k, (((1,), (1,)), ((), ())), preferred_element_type=F32)
    s = s + bias_ref[:, 2 * w - nk:]
    m = jnp.max(s, axis=-1, keepdims=True)
    p = jnp.exp2((s - m) * (ATTN_SCALE * LOG2_E)).astype(BF16)
    v_ones = jnp.concatenate([v, jnp.ones((nk, HEAD_DIM), BF16)], axis=1)
    pv = jnp.dot(p, v_ones, preferred_element_type=F32)
    den = pv[:, HEAD_DIM:]
    d_lo, d_hi = _stage_split(d)
    if d == 1:
        rows = pl.ds(n * w, w)
    elif d_hi == 1:
        rows = pl.ds(r + d * w * n, w, stride=d)
    else:
        r_lo, r_hi = r % d_lo, r // d_lo
        rows = pl.ds(r_lo * (seq // d_lo) + d_hi * w * n + r_hi, w, stride=d_hi)
    o_sc[rows, :] = pv[:, :HEAD_DIM] / den
    lse_sc[rows, :] = m * (ATTN_SCALE * LOG2_E) + jnp.log2(den)


def _unstage(staged_sc, natural_sc, d, seq):
    d_lo, _ = _stage_split(d)
    span = seq // d_lo
    for r_lo in range(d_lo):
        natural_sc[pl.ds(r_lo, span, stride=d_lo), :] = staged_sc[r_lo * span:(r_lo + 1) * span, :]


def _attn_kernel(*refs, seq):
    n_dil = len(DILATIONS)
    qkv_refs = refs[:3 * n_dil]
    o_ref = refs[3 * n_dil]
    bias_ref = refs[3 * n_dil + 1]
    stats = refs[3 * n_dil + 2:3 * n_dil + 2 + 2 * n_dil]
    staging = refs[3 * n_dil + 2 + 2 * n_dil:]
    w = ATTN_BLOCK

    qi = lax.broadcasted_iota(jnp.int32, (w, 2 * w), 0)
    kj = lax.broadcasted_iota(jnp.int32, (w, 2 * w), 1)
    bias_ref[...] = jnp.where((kj >= qi) & (kj <= qi + w), 0.0, -jnp.inf)

    for g, d in enumerate(DILATIONS):
        q_ref, k_ref, v_ref = qkv_refs[3 * g:3 * g + 3]
        o_sc, lse_sc = stats[2 * g:2 * g + 2]
        two_pass = _stage_split(d)[1] > 1
        o_dst, lse_dst = staging if two_pass else (o_sc, lse_sc)
        for r in range(d):
            for n in range(seq // d // w):
                _attn_block(q_ref, k_ref, v_ref, bias_ref, o_dst, lse_dst, d, r, n, seq)
        if two_pass:
            _unstage(o_dst, o_sc, d, seq)
            _unstage(lse_dst, lse_sc, d, seq)

    merge_rows = 256

    def merge_body(c, carry):
        rows = pl.ds(pl.multiple_of(c * merge_rows, merge_rows), merge_rows)
        lses = [stats[2 * g + 1][rows, :] for g in range(n_dil)]
        top = functools.reduce(jnp.maximum, lses)
        num = jnp.zeros((merge_rows, HEAD_DIM), F32)
        den = jnp.zeros((merge_rows, HEAD_DIM), F32)
        for g in range(n_dil):
            wgt = jnp.exp2(lses[g] - top)
            num = num + wgt * stats[2 * g][rows, :]
            den = den + wgt
        o_ref[rows, :] = (num / den).astype(o_ref.dtype)
        return carry

    lax.fori_loop(0, seq // merge_rows, merge_body, 0)


def _attention(qkv_groups, *, seq):
    batch = qkv_groups[0].shape[0]
    n_dil = len(DILATIONS)
    in_specs = []
    for g, d in enumerate(DILATIONS):
        for part in range(3):
            in_specs.append(pl.BlockSpec(
                (None, d, seq // d, HEAD_DIM),
                lambda bi, hi, part=part: (bi, 0, 0, part * HEADS + hi)))
    operands = [qkv_groups[g] for g in range(n_dil) for _ in range(3)]
    return pl.pallas_call(
        functools.partial(_attn_kernel, seq=seq),
        grid=(batch, HEADS),
        in_specs=in_specs,
        out_specs=pl.BlockSpec((None, seq, HEAD_DIM), lambda bi, hi: (bi, 0, hi)),
        out_shape=jax.ShapeDtypeStruct((batch, seq, HEADS * HEAD_DIM), BF16),
        scratch_shapes=[pltpu.VMEM((ATTN_BLOCK, 2 * ATTN_BLOCK), F32)]
        + [pltpu.VMEM((seq, HEAD_DIM), F32)] * (2 * n_dil + 2),
        compiler_params=_params(2),
        name="dilated_attn",
    )(*operands)


def _ffn_kernel(*refs, tm, seq, final_norm, mixer_proj):
    if mixer_proj:
        a_ref, wa_ref = refs[:2]
        refs = refs[2:]
    (h_hbm, gain_ref, wg_ref, wu_ref, cw_ref, cb_ref, wd_ref, fin_ref,
     o_ref, h_buf, h_sem, hn_ref, gbuf_ref, halo_ref) = refs
    i = pl.program_id(0)
    j = pl.program_id(1)

    def h_copy(tile):
        return pltpu.make_async_copy(h_hbm.at[pl.ds(tile * tm, tm), :], h_buf, h_sem)

    @pl.when((i == 0) & (j == 0))
    def _():
        h_copy(0).start()

    @pl.when(j == 0)
    def _():
        h_copy(i).wait()
        x = h_buf[...]
        if mixer_proj:
            x = x + jnp.dot(a_ref[...], wa_ref[...], preferred_element_type=F32)
        hn_ref[...] = _rms(x, gain_ref[...]).astype(BF16)
        o_ref[...] = x

    @pl.when((j == 1) & (i + 1 < pl.num_programs(0)))
    def _():
        h_copy(i + 1).start()

    hn = hn_ref[...]
    g = jnp.dot(hn, wg_ref[...], preferred_element_type=F32)
    u = jnp.dot(hn, wu_ref[...], preferred_element_type=F32)

    seq_start = (i * tm) % seq == 0
    gbuf_ref[0:8, :] = jnp.where(seq_start, 0.0, halo_ref[j])
    gbuf_ref[8:8 + tm, :] = g
    conv = (cw_ref[2:3, :] * g
            + cw_ref[1:2, :] * gbuf_ref[7:7 + tm, :]
            + cw_ref[0:1, :] * gbuf_ref[6:6 + tm, :]
            + cb_ref[...])
    act = (conv * (1.0 / (1.0 + jnp.exp(-conv))) * u).astype(BF16)
    o_ref[...] += jnp.dot(act, wd_ref[...], preferred_element_type=F32)
    halo_ref[j] = gbuf_ref[tm:tm + 8, :]

    if final_norm:
        @pl.when(j == pl.num_programs(1) - 1)
        def _():
            o_ref[...] = _rms(o_ref[...], fin_ref[...])


def _ffn(h, gain, w_gate, w_up, conv_w, conv_b, w_down, final_gain, *, seq,
         final_norm, mixer=None, tm=1024, tf=512):
    t, d = h.shape
    f = w_gate.shape[1]
    mixer_specs, mixer_args = [], []
    if mixer is not None:
        a, w_a = mixer
        mixer_args = [a, w_a]
        mixer_specs = [
            pl.BlockSpec((tm, a.shape[1]), lambda i, j: (i, 0)),
            pl.BlockSpec(w_a.shape, lambda i, j: (0, 0), pipeline_mode=pl.Buffered(1)),
        ]
    return pl.pallas_call(
        functools.partial(_ffn_kernel, tm=tm, seq=seq, final_norm=final_norm,
                          mixer_proj=mixer is not None),
        grid=(t // tm, f // tf),
        in_specs=mixer_specs + [
            pl.BlockSpec(memory_space=pl.ANY),
            pl.BlockSpec((1, d), lambda i, j: (0, 0)),
            pl.BlockSpec((d, tf), lambda i, j: (0, j)),
            pl.BlockSpec((d, tf), lambda i, j: (0, j)),
            pl.BlockSpec((conv_w.shape[0], tf), lambda i, j: (0, j)),
            pl.BlockSpec((1, tf), lambda i, j: (0, j)),
            pl.BlockSpec((tf, d), lambda i, j: (j, 0)),
            pl.BlockSpec((1, d), lambda i, j: (0, 0)),
        ],
        out_specs=pl.BlockSpec((tm, d), lambda i, j: (i, 0)),
        out_shape=jax.ShapeDtypeStruct((t, d), F32),
        scratch_shapes=[
            pltpu.VMEM((tm, d), F32),
            pltpu.SemaphoreType.DMA(()),
            pltpu.VMEM((tm, d), BF16),
            pltpu.VMEM((tm + 8, tf), F32),
            pltpu.VMEM((f // tf, 8, tf), F32),
        ],
        compiler_params=_params(2),
        name="conv_ffn",
    )(*mixer_args, h, gain, w_gate, w_up, conv_w, conv_b, w_down, final_gain)


def _rope_lane_order(x):
    half = ROT_DIM // 2
    lane = lax.broadcasted_iota(jnp.int32, x.shape, 1)
    up = pltpu.roll(x, HEAD_DIM - half, 1)
    down = pltpu.roll(x, HEAD_DIM // 2 - half, 1)
    return jnp.where(lane < half, x,
                     jnp.where(lane < HEAD_DIM // 2, up,
                               jnp.where(lane < HEAD_DIM // 2 + half, down, x)))


def _cast_kernel(w_ref, o_ref, *, rope_cols):
    cols = w_ref.shape[1]
    for c in range(rope_cols // HEAD_DIM):
        lanes = slice(c * HEAD_DIM, (c + 1) * HEAD_DIM)
        o_ref[:, lanes] = _rope_lane_order(w_ref[:, lanes]).astype(o_ref.dtype)
    if rope_cols < cols:
        o_ref[:, rope_cols:] = w_ref[:, rope_cols:].astype(o_ref.dtype)


def _cast_bf16(w_stack, layer, *, rope_cols=0):
    _, rows, cols = w_stack.shape
    tr = 512 if 512 * cols * 4 <= CAST_BLOCK_BYTES else 256
    return pl.pallas_call(
        functools.partial(_cast_kernel, rope_cols=rope_cols),
        grid=(rows // tr,),
        in_specs=[pl.BlockSpec((None, tr, cols), lambda i: (layer, i, 0))],
        out_specs=pl.BlockSpec((tr, cols), lambda i: (i, 0)),
        out_shape=jax.ShapeDtypeStruct((rows, cols), BF16),
        compiler_params=_params(1),
        name="cast_bf16_rope" if rope_cols else "cast_bf16",
    )(w_stack)


def _rope_tables(seq):
    half = ROT_DIM // 2
    pos = jnp.arange(seq, dtype=F32)
    inv = jnp.float32(ROPE_THETA) ** (-jnp.arange(0, ROT_DIM, 2, dtype=F32) / ROT_DIM)
    ang = pos[:, None] * inv[None, :]
    cos, sin = jnp.cos(ang), jnp.sin(ang)
    gap = jnp.zeros((seq, HEAD_DIM // 2 - half), F32)
    cos_t = jnp.concatenate([cos, gap + 1.0, cos, gap + 1.0], axis=1)
    sin_t = jnp.concatenate([-sin, gap, sin, gap], axis=1)
    return cos_t, sin_t


def kernel(x, a_ln, a_w_in, a_sgu_ln_g, a_sgu_ln_b, a_w_s, a_b_s, a_w_out, b_ln, b_w_qkv, b_w_o, ffn_ln, ffn_w_gate, ffn_w_up, ffn_conv_w, ffn_conv_b, ffn_w_down, final_ln):
    b, s, d = x.shape
    depth = ffn_ln.shape[0]
    h = x.reshape(b * s, d)
    tables = _rope_tables(s)
    for i in range(depth):
        j = i // 2
        mixer = None
        if i % 2 == 0:
            z = _rms_gelu_proj(h, a_ln[j][None], _cast_bf16(a_w_in, j))
            h = _gmlp_mix(z, a_sgu_ln_g[j][None], a_sgu_ln_b[j][None], a_w_s[j],
                          a_b_s[j].T, _cast_bf16(a_w_out, j), h)
        else:
            qk_cols = 2 * len(DILATIONS) * HEADS * HEAD_DIM
            w_qkv = _cast_bf16(b_w_qkv, j, rope_cols=qk_cols)
            qkv = [_rms_qkv(h, b_ln[j][None], w_qkv, tables, batch=b, seq=s, g=g)
                   for g in range(len(DILATIONS))]
            att = _attention(qkv, seq=s)
            mixer = (att.reshape(b * s, -1), _cast_bf16(b_w_o, j))
        h = _ffn(h, ffn_ln[i][None], _cast_bf16(ffn_w_gate, i), _cast_bf16(ffn_w_up, i),
                 ffn_conv_w[i], ffn_conv_b[i][None], _cast_bf16(ffn_w_down, i),
                 final_ln[None], seq=s, final_norm=(i == depth - 1), mixer=mixer)
    return h.reshape(b, s, d)
```

```python
import functools

import jax
import jax.numpy as jnp
from jax import lax
from jax.experimental import pallas as pl
from jax.experimental.pallas import tpu as pltpu

F32 = jnp.float32
BF16 = jnp.bfloat16

EPS = 1e-6
CHUNK = 128
HEAD_DIM = 128
HEADS = 8
DILATIONS = (1, 4, 16)
ATTN_BLOCK = 128
ROT_DIM = 32
ROPE_THETA = 500000.0
ATTN_SCALE = HEAD_DIM ** -0.5
LOG2_E = 1.4426950408889634
MXU_COLS = 256
MAX_ROW_STRIDE = 4
CAST_BLOCK_BYTES = 12 * 1024 * 1024

VMEM_LIMIT_BYTES = 56 * 1024 * 1024


def _params(n_axes):
    return pltpu.CompilerParams(
        dimension_semantics=("arbitrary",) * n_axes,
        vmem_limit_bytes=VMEM_LIMIT_BYTES,
    )


def _rms(x, gain):
    ms = jnp.mean(x * x, axis=-1, keepdims=True)
    return x * lax.rsqrt(ms + EPS) * gain


def _gelu_tanh(x):
    c = 0.7978845608028654
    return 0.5 * x * (1.0 + jnp.tanh(c * (x + 0.044715 * (x * x * x))))


def _rms_gelu_proj_kernel(h_ref, gain_ref, w_ref, o_ref, hn_ref):
    @pl.when(pl.program_id(1) == 0)
    def _():
        hn_ref[...] = _rms(h_ref[...], gain_ref[...]).astype(BF16)

    hn = hn_ref[...]
    for c in range(w_ref.shape[1] // MXU_COLS):
        cols = slice(c * MXU_COLS, (c + 1) * MXU_COLS)
        y = jnp.dot(hn, w_ref[:, cols], preferred_element_type=F32)
        o_ref[:, cols] = _gelu_tanh(y).astype(o_ref.dtype)


def _rms_gelu_proj(h, gain, w, *, tm=1024, tn=2048):
    t, d = h.shape
    n = w.shape[1]
    return pl.pallas_call(
        _rms_gelu_proj_kernel,
        grid=(t // tm, n // tn),
        in_specs=[
            pl.BlockSpec((tm, d), lambda i, j: (i, 0)),
            pl.BlockSpec((1, d), lambda i, j: (0, 0)),
            pl.BlockSpec((d, tn), lambda i, j: (0, j)),
        ],
        out_specs=pl.BlockSpec((tm, tn), lambda i, j: (i, j)),
        out_shape=jax.ShapeDtypeStruct((t, n), BF16),
        scratch_shapes=[pltpu.VMEM((tm, d), BF16)],
        compiler_params=_params(2),
        name="rms_gelu_proj",
    )(h, gain, w)


def _gmlp_mix_kernel(u_ref, v_ref, lng_ref, lnb_ref, ws_ref, bst_ref, wout_ref,
                     h_ref, o_ref, gated_a, gated_b, *, tm):
    s = pl.program_id(0)
    n_chunks = tm // CHUNK
    groups = ws_ref.shape[0]

    @pl.when(s == 0)
    def _():
        gated_b[...] = jnp.zeros_like(gated_b)

    def step(gated_new, gated_prev):
        d_out = o_ref.shape[1]
        n_proj = d_out // MXU_COLS
        assert groups % n_proj == 0, (groups, n_proj)
        groups_per_proj = groups // n_proj

        def project(c):
            cols = slice(c * MXU_COLS, (c + 1) * MXU_COLS)
            o_ref[:, cols] = h_ref[:, cols] + jnp.dot(
                gated_prev[...], wout_ref[:, cols], preferred_element_type=F32)

        v = v_ref[...].astype(F32)
        mu = jnp.mean(v, axis=-1, keepdims=True)
        vc = v - mu
        var = jnp.mean(vc * vc, axis=-1, keepdims=True)
        vn = (vc * lax.rsqrt(var + EPS) * lng_ref[...] + lnb_ref[...]).astype(BF16)

        row = lax.broadcasted_iota(jnp.int32, (CHUNK, CHUNK), 0)
        col = lax.broadcasted_iota(jnp.int32, (CHUNK, CHUNK), 1)
        causal = col <= row
        for g in range(groups):
            if g % groups_per_proj == 0 and g // groups_per_proj < n_proj:
                project(g // groups_per_proj)
            cols = slice(g * CHUNK, (g + 1) * CHUNK)
            rhs = jnp.concatenate(
                [vn[n * CHUNK:(n + 1) * CHUNK, cols] for n in range(n_chunks)], axis=1)
            ws = jnp.where(causal, ws_ref[g], 0.0).astype(BF16)
            mixed = jnp.dot(ws, rhs, preferred_element_type=F32) + bst_ref[:, g:g + 1]
            for n in range(n_chunks):
                rows = slice(n * CHUNK, (n + 1) * CHUNK)
                u = u_ref[rows, cols].astype(F32)
                gated_new[rows, cols] = (u * mixed[:, n * CHUNK:(n + 1) * CHUNK]).astype(BF16)

    @pl.when(s % 2 == 0)
    def _():
        step(gated_a, gated_b)

    @pl.when(s % 2 == 1)
    def _():
        step(gated_b, gated_a)


def _gmlp_mix(z, ln_g, ln_b, w_s, b_s_t, w_out, h, *, tm=512):
    t, d = h.shape
    width = w_out.shape[0]
    groups = w_s.shape[0]
    n = t // tm

    def gate_tile(s):
        return jnp.minimum(s, n - 1)

    def proj_tile(s):
        return jnp.maximum(s - 1, 0)

    return pl.pallas_call(
        functools.partial(_gmlp_mix_kernel, tm=tm),
        grid=(n + 1,),
        in_specs=[
            pl.BlockSpec((tm, width), lambda s: (gate_tile(s), 0)),
            pl.BlockSpec((tm, width), lambda s: (gate_tile(s), 1)),
            pl.BlockSpec((1, width), lambda s: (0, 0)),
            pl.BlockSpec((1, width), lambda s: (0, 0)),
            pl.BlockSpec((groups, CHUNK, CHUNK), lambda s: (0, 0, 0)),
            pl.BlockSpec((CHUNK, groups), lambda s: (0, 0)),
            pl.BlockSpec((width, d), lambda s: (0, 0)),
            pl.BlockSpec((tm, d), lambda s: (proj_tile(s), 0)),
        ],
        out_specs=pl.BlockSpec((tm, d), lambda s: (proj_tile(s), 0)),
        out_shape=jax.ShapeDtypeStruct((t, d), F32),
        scratch_shapes=[pltpu.VMEM((tm, width), BF16)] * 2,
        compiler_params=_params(1),
        name="gmlp_mix",
    )(z, z, ln_g, ln_b, w_s, b_s_t, w_out, h)


def _rms_qkv_kernel(h_ref, gain_ref, w_ref, cos_ref, sin_ref, *refs):
    n_dil = len(DILATIONS)
    o_refs = refs[:n_dil]
    hn_ref, y_sc, y2_sc = refs[n_dil:]
    tm = h_ref.shape[0]
    step = pl.program_id(1)

    @pl.when(step == 0)
    def _():
        hn_ref[...] = _rms(h_ref[...], gain_ref[...]).astype(BF16)

    def project(o_ref, d, rotary):
        rows = tm // d
        d_lo, d_hi = _stage_split(d)
        hn = hn_ref[...]
        for c in range(w_ref.shape[1] // MXU_COLS):
            y = jnp.dot(hn, w_ref[:, c * MXU_COLS:(c + 1) * MXU_COLS],
                        preferred_element_type=F32)
            for hh in range(MXU_COLS // HEAD_DIM):
                head = c * (MXU_COLS // HEAD_DIM) + hh
                lanes = slice(head * HEAD_DIM, (head + 1) * HEAD_DIM)
                t = y[:, hh * HEAD_DIM:(hh + 1) * HEAD_DIM]
                if rotary:
                    t = t * cos_ref[...] + pltpu.roll(t, HEAD_DIM // 2, 1) * sin_ref[...]
                if d == 1:
                    o_ref[0, :, lanes] = t.astype(o_ref.dtype)
                    continue
                src_sc = y_sc
                y_sc[head] = t
                if d_hi > 1:
                    for r_lo in range(d_lo):
                        y2_sc[head, r_lo * (tm // d_lo):(r_lo + 1) * (tm // d_lo), :] = (
                            y_sc[head, pl.ds(r_lo, tm // d_lo, stride=d_lo), :])
                    src_sc = y2_sc
                for r_lo in range(d_lo):
                    for r_hi in range(d_hi):
                        src = pl.ds(r_lo * (tm // d_lo) + r_hi, rows, stride=d_hi) if d_hi > 1 \
                            else pl.ds(r_lo, rows, stride=d_lo)
                        o_ref[r_hi * d_lo + r_lo, :, lanes] = src_sc[head, src, :].astype(
                            o_ref.dtype)

    for g, d in enumerate(DILATIONS):
        for part in range(3):
            pl.when(step == 3 * g + part)(
                functools.partial(project, o_refs[g], d, part < 2))


def _rms_qkv(h, gain, w_qkv, tables, *, batch, seq, tm=1024):
    t, d_model = h.shape
    part_cols = HEADS * HEAD_DIM
    n_dil = len(DILATIONS)
    tiles_per_seq = seq // tm
    table_spec = pl.BlockSpec((tm, HEAD_DIM), lambda i, j: (i % tiles_per_seq, 0))

    def out_spec(g, d):
        return pl.BlockSpec(
            (None, d, tm // d, part_cols),
            lambda i, j: (i // tiles_per_seq, 0, i % tiles_per_seq, jnp.clip(j - 3 * g, 0, 2)))

    return pl.pallas_call(
        _rms_qkv_kernel,
        grid=(t // tm, 3 * n_dil),
        in_specs=[
            pl.BlockSpec((tm, d_model), lambda i, j: (i, 0)),
            pl.BlockSpec((1, d_model), lambda i, j: (0, 0)),
            pl.BlockSpec((d_model, part_cols), lambda i, j: (0, (j % 3) * n_dil + j // 3)),
            table_spec, table_spec,
        ],
        out_specs=[out_spec(g, d) for g, d in enumerate(DILATIONS)],
        out_shape=[jax.ShapeDtypeStruct((batch, d, seq // d, 3 * part_cols), BF16)
                   for d in DILATIONS],
        scratch_shapes=[pltpu.VMEM((tm, d_model), BF16)]
        + [pltpu.VMEM((HEADS, tm, HEAD_DIM), F32)] * 2,
        compiler_params=_params(2),
        name="rms_qkv",
    )(h, gain, w_qkv, *tables)


def _stage_split(d):
    d_lo = min(d, MAX_ROW_STRIDE)
    return d_lo, d // d_lo


def _attn_block(q_ref, k_ref, v_ref, bias_ref, o_sc, lse_sc, d, r, n, seq):
    w = ATTN_BLOCK
    q = q_ref[r, n * w:(n + 1) * w, :]
    first = max(n - 1, 0)
    k = k_ref[r, first * w:(n + 1) * w, :]
    v = v_ref[r, first * w:(n + 1) * w, :]
    nk = k.shape[0]
    s = lax.dot_general(q, k, (((1,), (1,)), ((), ())), preferred_element_type=F32)
    s = s + bias_ref[:, 2 * w - nk:]
    m = jnp.max(s, axis=-1, keepdims=True)
    p = jnp.exp2((s - m) * (ATTN_SCALE * LOG2_E)).astype(BF16)
    v_ones = jnp.concatenate([v, jnp.ones((nk, HEAD_DIM), BF16)], axis=1)
    pv = jnp.dot(p, v_ones, preferred_element_type=F32)
    den = pv[:, HEAD_DIM:]
    d_lo, d_hi = _stage_split(d)
    if d == 1:
        rows = pl.ds(n * w, w)
    elif d_hi == 1:
        rows = pl.ds(r + d * w * n, w, stride=d)
    else:
        r_lo, r_hi = r % d_lo, r // d_lo
        rows = pl.ds(r_lo * (seq // d_lo) + d_hi * w * n + r_hi, w, stride=d_hi)
    o_sc[rows, :] = pv[:, :HEAD_DIM] / den
    lse_sc[rows, :] = m * (ATTN_SCALE * LOG2_E) + jnp.log2(den)


def _unstage(staged_sc, natural_sc, d, seq):
    d_lo, _ = _stage_split(d)
    span = seq // d_lo
    for r_lo in range(d_lo):
        natural_sc[pl.ds(r_lo, span, stride=d_lo), :] = staged_sc[r_lo * span:(r_lo + 1) * span, :]


def _attn_kernel(*refs, seq):
    n_dil = len(DILATIONS)
    qkv_refs = refs[:3 * n_dil]
    o_ref = refs[3 * n_dil]
    bias_ref = refs[3 * n_dil + 1]
    stats = refs[3 * n_dil + 2:3 * n_dil + 2 + 2 * n_dil]
    staging = refs[3 * n_dil + 2 + 2 * n_dil:]
    w = ATTN_BLOCK

    qi = lax.broadcasted_iota(jnp.int32, (w, 2 * w), 0)
    kj = lax.broadcasted_iota(jnp.int32, (w, 2 * w), 1)
    bias_ref[...] = jnp.where((kj >= qi) & (kj <= qi + w), 0.0, -jnp.inf)

    for g, d in enumerate(DILATIONS):
        q_ref, k_ref, v_ref = qkv_refs[3 * g:3 * g + 3]
        o_sc, lse_sc = stats[2 * g:2 * g + 2]
        two_pass = _stage_split(d)[1] > 1
        o_dst, lse_dst = staging if two_pass else (o_sc, lse_sc)
        for r in range(d):
            for n in range(seq // d // w):
                _attn_block(q_ref, k_ref, v_ref, bias_ref, o_dst, lse_dst, d, r, n, seq)
        if two_pass:
            _unstage(o_dst, o_sc, d, seq)
            _unstage(lse_dst, lse_sc, d, seq)

    merge_rows = 256

    def merge_body(c, carry):
        rows = pl.ds(pl.multiple_of(c * merge_rows, merge_rows), merge_rows)
        lses = [stats[2 * g + 1][rows, :] for g in range(n_dil)]
        top = functools.reduce(jnp.maximum, lses)
        num = jnp.zeros((merge_rows, HEAD_DIM), F32)
        den = jnp.zeros((merge_rows, HEAD_DIM), F32)
        for g in range(n_dil):
            wgt = jnp.exp2(lses[g] - top)
            num = num + wgt * stats[2 * g][rows, :]
            den = den + wgt
        o_ref[rows, :] = (num / den).astype(o_ref.dtype)
        return carry

    lax.fori_loop(0, seq // merge_rows, merge_body, 0)


def _attention(qkv_groups, *, seq):
    batch = qkv_groups[0].shape[0]
    n_dil = len(DILATIONS)
    in_specs = []
    for g, d in enumerate(DILATIONS):
        for part in range(3):
            in_specs.append(pl.BlockSpec(
                (None, d, seq // d, HEAD_DIM),
                lambda bi, hi, part=part: (bi, 0, 0, part * HEADS + hi)))
    operands = [qkv_groups[g] for g in range(n_dil) for _ in range(3)]
    return pl.pallas_call(
        functools.partial(_attn_kernel, seq=seq),
        grid=(batch, HEADS),
        in_specs=in_specs,
        out_specs=pl.BlockSpec((None, seq, HEAD_DIM), lambda bi, hi: (bi, 0, hi)),
        out_shape=jax.ShapeDtypeStruct((batch, seq, HEADS * HEAD_DIM), BF16),
        scratch_shapes=[pltpu.VMEM((ATTN_BLOCK, 2 * ATTN_BLOCK), F32)]
        + [pltpu.VMEM((seq, HEAD_DIM), F32)] * (2 * n_dil + 2),
        compiler_params=_params(2),
        name="dilated_attn",
    )(*operands)


def _ffn_kernel(*refs, tm, seq, final_norm, mixer_proj):
    if mixer_proj:
        a_ref, wa_ref = refs[:2]
        refs = refs[2:]
    (h_hbm, gain_ref, wg_ref, wu_ref, cw_ref, cb_ref, wd_ref, fin_ref,
     o_ref, h_buf, h_sem, hn_ref, gbuf_ref, halo_ref) = refs
    i = pl.program_id(0)
    j = pl.program_id(1)

    def h_copy(tile):
        return pltpu.make_async_copy(h_hbm.at[pl.ds(tile * tm, tm), :], h_buf, h_sem)

    @pl.when((i == 0) & (j == 0))
    def _():
        h_copy(0).start()

    @pl.when(j == 0)
    def _():
        h_copy(i).wait()
        x = h_buf[...]
        if mixer_proj:
            x = x + jnp.dot(a_ref[...], wa_ref[...], preferred_element_type=F32)
        hn_ref[...] = _rms(x, gain_ref[...]).astype(BF16)
        o_ref[...] = x

    @pl.when((j == 1) & (i + 1 < pl.num_programs(0)))
    def _():
        h_copy(i + 1).start()

    hn = hn_ref[...]
    g = jnp.dot(hn, wg_ref[...], preferred_element_type=F32)
    u = jnp.dot(hn, wu_ref[...], preferred_element_type=F32)

    seq_start = (i * tm) % seq == 0
    gbuf_ref[0:8, :] = jnp.where(seq_start, 0.0, halo_ref[j])
    gbuf_ref[8:8 + tm, :] = g
    conv = (cw_ref[2:3, :] * g
            + cw_ref[1:2, :] * gbuf_ref[7:7 + tm, :]
            + cw_ref[0:1, :] * gbuf_ref[6:6 + tm, :]
            + cb_ref[...])
    act = (conv * (1.0 / (1.0 + jnp.exp(-conv))) * u).astype(BF16)
    o_ref[...] += jnp.dot(act, wd_ref[...], preferred_element_type=F32)
    halo_ref[j] = gbuf_ref[tm:tm + 8, :]

    if final_norm:
        @pl.when(j == pl.num_programs(1) - 1)
        def _():
            o_ref[...] = _rms(o_ref[...], fin_ref[...])


def _ffn(h, gain, w_gate, w_up, conv_w, conv_b, w_down, final_gain, *, seq,
         final_norm, mixer=None, tm=1024, tf=512):
    t, d = h.shape
    f = w_gate.shape[1]
    mixer_specs, mixer_args = [], []
    if mixer is not None:
        a, w_a = mixer
        mixer_args = [a, w_a]
        mixer_specs = [
            pl.BlockSpec((tm, a.shape[1]), lambda i, j: (i, 0)),
            pl.BlockSpec(w_a.shape, lambda i, j: (0, 0), pipeline_mode=pl.Buffered(1)),
        ]
    return pl.pallas_call(
        functools.partial(_ffn_kernel, tm=tm, seq=seq, final_norm=final_norm,
                          mixer_proj=mixer is not None),
        grid=(t // tm, f // tf),
        in_specs=mixer_specs + [
            pl.BlockSpec(memory_space=pl.ANY),
            pl.BlockSpec((1, d), lambda i, j: (0, 0)),
            pl.BlockSpec((d, tf), lambda i, j: (0, j)),
            pl.BlockSpec((d, tf), lambda i, j: (0, j)),
            pl.BlockSpec((conv_w.shape[0], tf), lambda i, j: (0, j)),
            pl.BlockSpec((1, tf), lambda i, j: (0, j)),
            pl.BlockSpec((tf, d), lambda i, j: (j, 0)),
            pl.BlockSpec((1, d), lambda i, j: (0, 0)),
        ],
        out_specs=pl.BlockSpec((tm, d), lambda i, j: (i, 0)),
        out_shape=jax.ShapeDtypeStruct((t, d), F32),
        scratch_shapes=[
            pltpu.VMEM((tm, d), F32),
            pltpu.SemaphoreType.DMA(()),
            pltpu.VMEM((tm, d), BF16),
            pltpu.VMEM((tm + 8, tf), F32),
            pltpu.VMEM((f // tf, 8, tf), F32),
        ],
        compiler_params=_params(2),
        name="conv_ffn",
    )(*mixer_args, h, gain, w_gate, w_up, conv_w, conv_b, w_down, final_gain)


def _rope_lane_order(x):
    half = ROT_DIM // 2
    lane = lax.broadcasted_iota(jnp.int32, x.shape, 1)
    up = pltpu.roll(x, HEAD_DIM - half, 1)
    down = pltpu.roll(x, HEAD_DIM // 2 - half, 1)
    return jnp.where(lane < half, x,
                     jnp.where(lane < HEAD_DIM // 2, up,
                               jnp.where(lane < HEAD_DIM // 2 + half, down, x)))


def _cast_kernel(w_ref, o_ref, *, rope_cols):
    cols = w_ref.shape[1]
    for c in range(rope_cols // HEAD_DIM):
        lanes = slice(c * HEAD_DIM, (c + 1) * HEAD_DIM)
        o_ref[:, lanes] = _rope_lane_order(w_ref[:, lanes]).astype(o_ref.dtype)
    if rope_cols < cols:
        o_ref[:, rope_cols:] = w_ref[:, rope_cols:].astype(o_ref.dtype)


def _cast_bf16(w_stack, layer, *, rope_cols=0):
    _, rows, cols = w_stack.shape
    tr = 512 if 512 * cols * 4 <= CAST_BLOCK_BYTES else 256
    return pl.pallas_call(
        functools.partial(_cast_kernel, rope_cols=rope_cols),
        grid=(rows // tr,),
        in_specs=[pl.BlockSpec((None, tr, cols), lambda i: (layer, i, 0))],
        out_specs=pl.BlockSpec((tr, cols), lambda i: (i, 0)),
        out_shape=jax.ShapeDtypeStruct((rows, cols), BF16),
        compiler_params=_params(1),
        name="cast_bf16_rope" if rope_cols else "cast_bf16",
    )(w_stack)


def _rope_tables(seq):
    half = ROT_DIM // 2
    pos = jnp.arange(seq, dtype=F32)
    inv = jnp.float32(ROPE_THETA) ** (-jnp.arange(0, ROT_DIM, 2, dtype=F32) / ROT_DIM)
    ang = pos[:, None] * inv[None, :]
    cos, sin = jnp.cos(ang), jnp.sin(ang)
    gap = jnp.zeros((seq, HEAD_DIM // 2 - half), F32)
    cos_t = jnp.concatenate([cos, gap + 1.0, cos, gap + 1.0], axis=1)
    sin_t = jnp.concatenate([-sin, gap, sin, gap], axis=1)
    return cos_t, sin_t


def kernel(x, a_ln, a_w_in, a_sgu_ln_g, a_sgu_ln_b, a_w_s, a_b_s, a_w_out, b_ln, b_w_qkv, b_w_o, ffn_ln, ffn_w_gate, ffn_w_up, ffn_conv_w, ffn_conv_b, ffn_w_down, final_ln):
    b, s, d = x.shape
    depth = ffn_ln.shape[0]
    h = x.reshape(b * s, d)
    tables = _rope_tables(s)
    for i in range(depth):
        j = i // 2
        mixer = None
        if i % 2 == 0:
            z = _rms_gelu_proj(h, a_ln[j][None], _cast_bf16(a_w_in, j))
            h = _gmlp_mix(z, a_sgu_ln_g[j][None], a_sgu_ln_b[j][None], a_w_s[j],
                          a_b_s[j].T, _cast_bf16(a_w_out, j), h)
        else:
            qk_cols = 2 * len(DILATIONS) * HEADS * HEAD_DIM
            w_qkv = _cast_bf16(b_w_qkv, j, rope_cols=qk_cols)
            qkv = _rms_qkv(h, b_ln[j][None], w_qkv, tables, batch=b, seq=s)
            att = _attention(qkv, seq=s)
            mixer = (att.reshape(b * s, -1), _cast_bf16(b_w_o, j))
        h = _ffn(h, ffn_ln[i][None], _cast_bf16(ffn_w_gate, i), _cast_bf16(ffn_w_up, i),
                 ffn_conv_w[i], ffn_conv_b[i][None], _cast_bf16(ffn_w_down, i),
                 final_ln[None], seq=s, final_norm=(i == depth - 1), mixer=mixer)
    return h.reshape(b, s, d)
```

```python
import functools

import jax
import jax.numpy as jnp
from jax import lax
from jax.experimental import pallas as pl
from jax.experimental.pallas import tpu as pltpu

F32 = jnp.float32
BF16 = jnp.bfloat16

EPS = 1e-6
CHUNK = 128
HEAD_DIM = 128
HEADS = 8
DILATIONS = (1, 4, 16)
ATTN_BLOCK = 128
ROT_DIM = 32
ROPE_THETA = 500000.0
ATTN_SCALE = HEAD_DIM ** -0.5
LOG2_E = 1.4426950408889634
MXU_COLS = 256
MAX_ROW_STRIDE = 4
CAST_BLOCK_BYTES = 12 * 1024 * 1024

VMEM_LIMIT_BYTES = 56 * 1024 * 1024


def _params(n_axes):
    return pltpu.CompilerParams(
        dimension_semantics=("arbitrary",) * n_axes,
        vmem_limit_bytes=VMEM_LIMIT_BYTES,
    )


def _rms(x, gain):
    ms = jnp.mean(x * x, axis=-1, keepdims=True)
    return x * lax.rsqrt(ms + EPS) * gain


def _gelu_tanh(x):
    c = 0.7978845608028654
    return 0.5 * x * (1.0 + jnp.tanh(c * (x + 0.044715 * (x * x * x))))


def _rms_gelu_proj_kernel(h_ref, gain_ref, w_ref, o_ref, hn_ref):
    def project():
        hn = hn_ref[...]
        for c in range(w_ref.shape[1] // MXU_COLS):
            cols = slice(c * MXU_COLS, (c + 1) * MXU_COLS)
            y = jnp.dot(hn, w_ref[:, cols], preferred_element_type=F32)
            o_ref[:, cols] = _gelu_tanh(y).astype(o_ref.dtype)

    @pl.when(pl.program_id(1) == 0)
    def _():
        hn_ref[...] = _rms(h_ref[...], gain_ref[...]).astype(BF16)
        project()

    @pl.when(pl.program_id(1) > 0)
    def _():
        project()


def _rms_gelu_proj(h, gain, w, *, tm=1024, tn=2048):
    t, d = h.shape
    n = w.shape[1]
    return pl.pallas_call(
        _rms_gelu_proj_kernel,
        grid=(t // tm, n // tn),
        in_specs=[
            pl.BlockSpec((tm, d), lambda i, j: (i, 0)),
            pl.BlockSpec((1, d), lambda i, j: (0, 0)),
            pl.BlockSpec((d, tn), lambda i, j: (0, j)),
        ],
        out_specs=pl.BlockSpec((tm, tn), lambda i, j: (i, j)),
        out_shape=jax.ShapeDtypeStruct((t, n), BF16),
        scratch_shapes=[pltpu.VMEM((tm, d), BF16)],
        compiler_params=_params(2),
        name="rms_gelu_proj",
    )(h, gain, w)


def _gmlp_mix_kernel(u_ref, v_ref, lng_ref, lnb_ref, ws_ref, bst_ref, wout_ref,
                     h_ref, o_ref, gated_a, gated_b, *, tm):
    s = pl.program_id(0)
    n_chunks = tm // CHUNK
    groups = ws_ref.shape[0]

    @pl.when(s == 0)
    def _():
        gated_b[...] = jnp.zeros_like(gated_b)

    def step(gated_new, gated_prev):
        d_out = o_ref.shape[1]
        n_proj = d_out // MXU_COLS
        assert groups % n_proj == 0, (groups, n_proj)
        groups_per_proj = groups // n_proj

        def project(c):
            cols = slice(c * MXU_COLS, (c + 1) * MXU_COLS)
            o_ref[:, cols] = h_ref[:, cols] + jnp.dot(
                gated_prev[...], wout_ref[:, cols], preferred_element_type=F32)

        v = v_ref[...].astype(F32)
        mu = jnp.mean(v, axis=-1, keepdims=True)
        vc = v - mu
        var = jnp.mean(vc * vc, axis=-1, keepdims=True)
        vn = (vc * lax.rsqrt(var + EPS) * lng_ref[...] + lnb_ref[...]).astype(BF16)

        row = lax.broadcasted_iota(jnp.int32, (CHUNK, CHUNK), 0)
        col = lax.broadcasted_iota(jnp.int32, (CHUNK, CHUNK), 1)
        causal = col <= row
        for g in range(groups):
            if g % groups_per_proj == 0 and g // groups_per_proj < n_proj:
                project(g // groups_per_proj)
            cols = slice(g * CHUNK, (g + 1) * CHUNK)
            rhs = jnp.concatenate(
                [vn[n * CHUNK:(n + 1) * CHUNK, cols] for n in range(n_chunks)], axis=1)
            ws = jnp.where(causal, ws_ref[g], 0.0).astype(BF16)
            mixed = jnp.dot(ws, rhs, preferred_element_type=F32) + bst_ref[:, g:g + 1]
            for n in range(n_chunks):
                rows = slice(n * CHUNK, (n + 1) * CHUNK)
                u = u_ref[rows, cols].astype(F32)
                gated_new[rows, cols] = (u * mixed[:, n * CHUNK:(n + 1) * CHUNK]).astype(BF16)

    @pl.when(s % 2 == 0)
    def _():
        step(gated_a, gated_b)

    @pl.when(s % 2 == 1)
    def _():
        step(gated_b, gated_a)


def _gmlp_mix(z, ln_g, ln_b, w_s, b_s_t, w_out, h, *, tm=512):
    t, d = h.shape
    width = w_out.shape[0]
    groups = w_s.shape[0]
    n = t // tm

    def gate_tile(s):
        return jnp.minimum(s, n - 1)

    def proj_tile(s):
        return jnp.maximum(s - 1, 0)

    return pl.pallas_call(
        functools.partial(_gmlp_mix_kernel, tm=tm),
        grid=(n + 1,),
        in_specs=[
            pl.BlockSpec((tm, width), lambda s: (gate_tile(s), 0)),
            pl.BlockSpec((tm, width), lambda s: (gate_tile(s), 1)),
            pl.BlockSpec((1, width), lambda s: (0, 0)),
            pl.BlockSpec((1, width), lambda s: (0, 0)),
            pl.BlockSpec((groups, CHUNK, CHUNK), lambda s: (0, 0, 0)),
            pl.BlockSpec((CHUNK, groups), lambda s: (0, 0)),
            pl.BlockSpec((width, d), lambda s: (0, 0)),
            pl.BlockSpec((tm, d), lambda s: (proj_tile(s), 0)),
        ],
        out_specs=pl.BlockSpec((tm, d), lambda s: (proj_tile(s), 0)),
        out_shape=jax.ShapeDtypeStruct((t, d), F32),
        scratch_shapes=[pltpu.VMEM((tm, width), BF16)] * 2,
        compiler_params=_params(1),
        name="gmlp_mix",
    )(z, z, ln_g, ln_b, w_s, b_s_t, w_out, h)


def _rms_qkv_kernel(h_ref, gain_ref, w_ref, cos_ref, sin_ref, *refs):
    n_dil = len(DILATIONS)
    o_refs = refs[:n_dil]
    hn_ref, y_sc, y2_sc = refs[n_dil:]
    tm = h_ref.shape[0]
    step = pl.program_id(1)

    @pl.when(step == 0)
    def _():
        hn_ref[...] = _rms(h_ref[...], gain_ref[...]).astype(BF16)

    def project(o_ref, d, rotary):
        rows = tm // d
        d_lo, d_hi = _stage_split(d)
        hn = hn_ref[...]
        for c in range(w_ref.shape[1] // MXU_COLS):
            y = jnp.dot(hn, w_ref[:, c * MXU_COLS:(c + 1) * MXU_COLS],
                        preferred_element_type=F32)
            for hh in range(MXU_COLS // HEAD_DIM):
                head = c * (MXU_COLS // HEAD_DIM) + hh
                lanes = slice(head * HEAD_DIM, (head + 1) * HEAD_DIM)
                t = y[:, hh * HEAD_DIM:(hh + 1) * HEAD_DIM]
                if rotary:
                    t = t * cos_ref[...] + pltpu.roll(t, HEAD_DIM // 2, 1) * sin_ref[...]
                if d == 1:
                    o_ref[0, :, lanes] = t.astype(o_ref.dtype)
                    continue
                src_sc = y_sc
                y_sc[head] = t
                if d_hi > 1:
                    for r_lo in range(d_lo):
                        y2_sc[head, r_lo * (tm // d_lo):(r_lo + 1) * (tm // d_lo), :] = (
                            y_sc[head, pl.ds(r_lo, tm // d_lo, stride=d_lo), :])
                    src_sc = y2_sc
                for r_lo in range(d_lo):
                    for r_hi in range(d_hi):
                        src = pl.ds(r_lo * (tm // d_lo) + r_hi, rows, stride=d_hi) if d_hi > 1 \
                            else pl.ds(r_lo, rows, stride=d_lo)
                        o_ref[r_hi * d_lo + r_lo, :, lanes] = src_sc[head, src, :].astype(
                            o_ref.dtype)

    for g, d in enumerate(DILATIONS):
        for part in range(3):
            pl.when(step == 3 * g + part)(
                functools.partial(project, o_refs[g], d, part < 2))


def _rms_qkv(h, gain, w_qkv, tables, *, batch, seq, tm=1024):
    t, d_model = h.shape
    part_cols = HEADS * HEAD_DIM
    n_dil = len(DILATIONS)
    tiles_per_seq = seq // tm
    table_spec = pl.BlockSpec((tm, HEAD_DIM), lambda i, j: (i % tiles_per_seq, 0))

    def out_spec(g, d):
        return pl.BlockSpec(
            (None, d, tm // d, part_cols),
            lambda i, j: (i // tiles_per_seq, 0, i % tiles_per_seq, jnp.clip(j - 3 * g, 0, 2)))

    return pl.pallas_call(
        _rms_qkv_kernel,
        grid=(t // tm, 3 * n_dil),
        in_specs=[
            pl.BlockSpec((tm, d_model), lambda i, j: (i, 0)),
            pl.BlockSpec((1, d_model), lambda i, j: (0, 0)),
            pl.BlockSpec((d_model, part_cols), lambda i, j: (0, (j % 3) * n_dil + j // 3)),
            table_spec, table_spec,
        ],
        out_specs=[out_spec(g, d) for g, d in enumerate(DILATIONS)],
        out_shape=[jax.ShapeDtypeStruct((batch, d, seq // d, 3 * part_cols), BF16)
                   for d in DILATIONS],
        scratch_shapes=[pltpu.VMEM((tm, d_model), BF16)]
        + [pltpu.VMEM((HEADS, tm, HEAD_DIM), F32)] * 2,
        compiler_params=_params(2),
        name="rms_qkv",
    )(h, gain, w_qkv, *tables)


def _stage_split(d):
    d_lo = min(d, MAX_ROW_STRIDE)
    return d_lo, d // d_lo


def _attn_block(q_ref, k_ref, v_ref, bias_ref, o_sc, lse_sc, d, r, n, seq):
    w = ATTN_BLOCK
    q = q_ref[r, n * w:(n + 1) * w, :]
    first = max(n - 1, 0)
    k = k_ref[r, first * w:(n + 1) * w, :]
    v = v_ref[r, first * w:(n + 1) * w, :]
    nk = k.shape[0]
    s = lax.dot_general(q, k, (((1,), (1,)), ((), ())), preferred_element_type=F32)
    s = s + bias_ref[:, 2 * w - nk:]
    m = jnp.max(s, axis=-1, keepdims=True)
    p = jnp.exp2((s - m) * (ATTN_SCALE * LOG2_E)).astype(BF16)
    v_ones = jnp.concatenate([v, jnp.ones((nk, HEAD_DIM), BF16)], axis=1)
    pv = jnp.dot(p, v_ones, preferred_element_type=F32)
    den = pv[:, HEAD_DIM:]
    d_lo, d_hi = _stage_split(d)
    if d == 1:
        rows = pl.ds(n * w, w)
    elif d_hi == 1:
        rows = pl.ds(r + d * w * n, w, stride=d)
    else:
        r_lo, r_hi = r % d_lo, r // d_lo
        rows = pl.ds(r_lo * (seq // d_lo) + d_hi * w * n + r_hi, w, stride=d_hi)
    o_sc[rows, :] = pv[:, :HEAD_DIM] / den
    lse_sc[rows, :] = m * (ATTN_SCALE * LOG2_E) + jnp.log2(den)


def _unstage(staged_sc, natural_sc, d, seq):
    d_lo, _ = _stage_split(d)
    span = seq // d_lo
    for r_lo in range(d_lo):
        natural_sc[pl.ds(r_lo, span, stride=d_lo), :] = staged_sc[r_lo * span:(r_lo + 1) * span, :]


def _attn_kernel(*refs, seq):
    n_dil = len(DILATIONS)
    qkv_refs = refs[:3 * n_dil]
    o_ref = refs[3 * n_dil]
    bias_ref = refs[3 * n_dil + 1]
    stats = refs[3 * n_dil + 2:3 * n_dil + 2 + 2 * n_dil]
    staging = refs[3 * n_dil + 2 + 2 * n_dil:]
    w = ATTN_BLOCK

    qi = lax.broadcasted_iota(jnp.int32, (w, 2 * w), 0)
    kj = lax.broadcasted_iota(jnp.int32, (w, 2 * w), 1)
    bias_ref[...] = jnp.where((kj >= qi) & (kj <= qi + w), 0.0, -jnp.inf)

    for g, d in enumerate(DILATIONS):
        q_ref, k_ref, v_ref = qkv_refs[3 * g:3 * g + 3]
        o_sc, lse_sc = stats[2 * g:2 * g + 2]
        two_pass = _stage_split(d)[1] > 1
        o_dst, lse_dst = staging if two_pass else (o_sc, lse_sc)
        for r in range(d):
            for n in range(seq // d // w):
                _attn_block(q_ref, k_ref, v_ref, bias_ref, o_dst, lse_dst, d, r, n, seq)
        if two_pass:
            _unstage(o_dst, o_sc, d, seq)
            _unstage(lse_dst, lse_sc, d, seq)

    merge_rows = 256

    def merge_body(c, carry):
        rows = pl.ds(pl.multiple_of(c * merge_rows, merge_rows), merge_rows)
        lses = [stats[2 * g + 1][rows, :] for g in range(n_dil)]
        top = functools.reduce(jnp.maximum, lses)
        num = jnp.zeros((merge_rows, HEAD_DIM), F32)
        den = jnp.zeros((merge_rows, HEAD_DIM), F32)
        for g in range(n_dil):
            wgt = jnp.exp2(lses[g] - top)
            num = num + wgt * stats[2 * g][rows, :]
            den = den + wgt
        o_ref[rows, :] = (num / den).astype(o_ref.dtype)
        return carry

    lax.fori_loop(0, seq // merge_rows, merge_body, 0)


def _attention(qkv_groups, *, seq):
    batch = qkv_groups[0].shape[0]
    n_dil = len(DILATIONS)
    in_specs = []
    for g, d in enumerate(DILATIONS):
        for part in range(3):
            in_specs.append(pl.BlockSpec(
                (None, d, seq // d, HEAD_DIM),
                lambda bi, hi, part=part: (bi, 0, 0, part * HEADS + hi)))
    operands = [qkv_groups[g] for g in range(n_dil) for _ in range(3)]
    return pl.pallas_call(
        functools.partial(_attn_kernel, seq=seq),
        grid=(batch, HEADS),
        in_specs=in_specs,
        out_specs=pl.BlockSpec((None, seq, HEAD_DIM), lambda bi, hi: (bi, 0, hi)),
        out_shape=jax.ShapeDtypeStruct((batch, seq, HEADS * HEAD_DIM), BF16),
        scratch_shapes=[pltpu.VMEM((ATTN_BLOCK, 2 * ATTN_BLOCK), F32)]
        + [pltpu.VMEM((seq, HEAD_DIM), F32)] * (2 * n_dil + 2),
        compiler_params=_params(2),
        name="dilated_attn",
    )(*operands)


def _ffn_kernel(*refs, tm, seq, final_norm, mixer_proj):
    if mixer_proj:
        a_ref, wa_ref = refs[:2]
        refs = refs[2:]
    (h_hbm, gain_ref, wg_ref, wu_ref, cw_ref, cb_ref, wd_ref, fin_ref,
     o_ref, h_buf, h_sem, hn_ref, gbuf_ref, halo_ref) = refs
    i = pl.program_id(0)
    j = pl.program_id(1)

    def h_copy(tile):
        return pltpu.make_async_copy(h_hbm.at[pl.ds(tile * tm, tm), :], h_buf, h_sem)

    @pl.when((i == 0) & (j == 0))
    def _():
        h_copy(0).start()

    def ffn_chunk():
        hn = hn_ref[...]
        g = jnp.dot(hn, wg_ref[...], preferred_element_type=F32)
        u = jnp.dot(hn, wu_ref[...], preferred_element_type=F32)

        seq_start = (i * tm) % seq == 0
        gbuf_ref[0:8, :] = jnp.where(seq_start, 0.0, halo_ref[j])
        gbuf_ref[8:8 + tm, :] = g
        conv = (cw_ref[2:3, :] * g
                + cw_ref[1:2, :] * gbuf_ref[7:7 + tm, :]
                + cw_ref[0:1, :] * gbuf_ref[6:6 + tm, :]
                + cb_ref[...])
        act = (conv * (1.0 / (1.0 + jnp.exp(-conv))) * u).astype(BF16)
        o_ref[...] += jnp.dot(act, wd_ref[...], preferred_element_type=F32)
        halo_ref[j] = gbuf_ref[tm:tm + 8, :]

    @pl.when(j == 0)
    def _():
        h_copy(i).wait()
        x = h_buf[...]
        if mixer_proj:
            x = x + jnp.dot(a_ref[...], wa_ref[...], preferred_element_type=F32)
        hn_ref[...] = _rms(x, gain_ref[...]).astype(BF16)
        o_ref[...] = x
        ffn_chunk()

    @pl.when((j == 1) & (i + 1 < pl.num_programs(0)))
    def _():
        h_copy(i + 1).start()

    @pl.when(j > 0)
    def _():
        ffn_chunk()

    if final_norm:
        @pl.when(j == pl.num_programs(1) - 1)
        def _():
            o_ref[...] = _rms(o_ref[...], fin_ref[...])


def _ffn(h, gain, w_gate, w_up, conv_w, conv_b, w_down, final_gain, *, seq,
         final_norm, mixer=None, tm=1024, tf=512):
    t, d = h.shape
    f = w_gate.shape[1]
    mixer_specs, mixer_args = [], []
    if mixer is not None:
        a, w_a = mixer
        mixer_args = [a, w_a]
        mixer_specs = [
            pl.BlockSpec((tm, a.shape[1]), lambda i, j: (i, 0)),
            pl.BlockSpec(w_a.shape, lambda i, j: (0, 0), pipeline_mode=pl.Buffered(1)),
        ]
    return pl.pallas_call(
        functools.partial(_ffn_kernel, tm=tm, seq=seq, final_norm=final_norm,
                          mixer_proj=mixer is not None),
        grid=(t // tm, f // tf),
        in_specs=mixer_specs + [
            pl.BlockSpec(memory_space=pl.ANY),
            pl.BlockSpec((1, d), lambda i, j: (0, 0)),
            pl.BlockSpec((d, tf), lambda i, j: (0, j)),
            pl.BlockSpec((d, tf), lambda i, j: (0, j)),
            pl.BlockSpec((conv_w.shape[0], tf), lambda i, j: (0, j)),
            pl.BlockSpec((1, tf), lambda i, j: (0, j)),
            pl.BlockSpec((tf, d), lambda i, j: (j, 0)),
            pl.BlockSpec((1, d), lambda i, j: (0, 0)),
        ],
        out_specs=pl.BlockSpec((tm, d), lambda i, j: (i, 0)),
        out_shape=jax.ShapeDtypeStruct((t, d), F32),
        scratch_shapes=[
            pltpu.VMEM((tm, d), F32),
            pltpu.SemaphoreType.DMA(()),
            pltpu.VMEM((tm, d), BF16),
            pltpu.VMEM((tm + 8, tf), F32),
            pltpu.VMEM((f // tf, 8, tf), F32),
        ],
        compiler_params=_params(2),
        name="conv_ffn",
    )(*mixer_args, h, gain, w_gate, w_up, conv_w, conv_b, w_down, final_gain)


def _rope_lane_order(x):
    half = ROT_DIM // 2
    lane = lax.broadcasted_iota(jnp.int32, x.shape, 1)
    up = pltpu.roll(x, HEAD_DIM - half, 1)
    down = pltpu.roll(x, HEAD_DIM // 2 - half, 1)
    return jnp.where(lane < half, x,
                     jnp.where(lane < HEAD_DIM // 2, up,
                               jnp.where(lane < HEAD_DIM // 2 + half, down, x)))


def _cast_kernel(w_ref, o_ref, *, rope_cols):
    cols = w_ref.shape[1]
    for c in range(rope_cols // HEAD_DIM):
        lanes = slice(c * HEAD_DIM, (c + 1) * HEAD_DIM)
        o_ref[:, lanes] = _rope_lane_order(w_ref[:, lanes]).astype(o_ref.dtype)
    if rope_cols < cols:
        o_ref[:, rope_cols:] = w_ref[:, rope_cols:].astype(o_ref.dtype)


def _cast_bf16(w_stack, layer, *, rope_cols=0):
    _, rows, cols = w_stack.shape
    tr = 512 if 512 * cols * 4 <= CAST_BLOCK_BYTES else 256
    return pl.pallas_call(
        functools.partial(_cast_kernel, rope_cols=rope_cols),
        grid=(rows // tr,),
        in_specs=[pl.BlockSpec((None, tr, cols), lambda i: (layer, i, 0))],
        out_specs=pl.BlockSpec((tr, cols), lambda i: (i, 0)),
        out_shape=jax.ShapeDtypeStruct((rows, cols), BF16),
        compiler_params=_params(1),
        name="cast_bf16_rope" if rope_cols else "cast_bf16",
    )(w_stack)


def _rope_tables(seq):
    half = ROT_DIM // 2
    pos = jnp.arange(seq, dtype=F32)
    inv = jnp.float32(ROPE_THETA) ** (-jnp.arange(0, ROT_DIM, 2, dtype=F32) / ROT_DIM)
    ang = pos[:, None] * inv[None, :]
    cos, sin = jnp.cos(ang), jnp.sin(ang)
    gap = jnp.zeros((seq, HEAD_DIM // 2 - half), F32)
    cos_t = jnp.concatenate([cos, gap + 1.0, cos, gap + 1.0], axis=1)
    sin_t = jnp.concatenate([-sin, gap, sin, gap], axis=1)
    return cos_t, sin_t


def kernel(x, a_ln, a_w_in, a_sgu_ln_g, a_sgu_ln_b, a_w_s, a_b_s, a_w_out, b_ln, b_w_qkv, b_w_o, ffn_ln, ffn_w_gate, ffn_w_up, ffn_conv_w, ffn_conv_b, ffn_w_down, final_ln):
    b, s, d = x.shape
    depth = ffn_ln.shape[0]
    h = x.reshape(b * s, d)
    tables = _rope_tables(s)
    for i in range(depth):
        j = i // 2
        mixer = None
        if i % 2 == 0:
            z = _rms_gelu_proj(h, a_ln[j][None], _cast_bf16(a_w_in, j))
            h = _gmlp_mix(z, a_sgu_ln_g[j][None], a_sgu_ln_b[j][None], a_w_s[j],
                          a_b_s[j].T, _cast_bf16(a_w_out, j), h)
        else:
            qk_cols = 2 * len(DILATIONS) * HEADS * HEAD_DIM
            w_qkv = _cast_bf16(b_w_qkv, j, rope_cols=qk_cols)
            qkv = _rms_qkv(h, b_ln[j][None], w_qkv, tables, batch=b, seq=s)
            att = _attention(qkv, seq=s)
            mixer = (att.reshape(b * s, -1), _cast_bf16(b_w_o, j))
        h = _ffn(h, ffn_ln[i][None], _cast_bf16(ffn_w_gate, i), _cast_bf16(ffn_w_up, i),
                 ffn_conv_w[i], ffn_conv_b[i][None], _cast_bf16(ffn_w_down, i),
                 final_ln[None], seq=s, final_norm=(i == depth - 1), mixer=mixer)
    return h.reshape(b, s, d)
```
